```python
import math
import jax
import jax.numpy as jnp
from jax import lax
import numpy as np

D_MODEL = 1024
BATCH = 4
SEQ = 8192
DEPTH = 4

PLE_DIM = 256
D_FF = 2816
HEAD_DIM = 64
QBLK = 128
NUM_BUCKETS = 32
MAX_DISTANCE = 128
NEG = -1e30
FORCE = 1e6
EPS = 1e-6

NSA_HEADS = 6
NSA_KV_HEADS = 2
NSA_GROUP = NSA_HEADS // NSA_KV_HEADS
CMP_BLOCK = 32
CMP_STRIDE = 16
CMP_HIDDEN = 256
SLC_BLOCK = 64
N_SELECT = 16
WINDOW = 512

DIFF_HEADS = 4
DIFF_QK_DIM = HEAD_DIM // 2

HGRN_HEADS = 6
HGRN_KEY_DIM = 64
HGRN_VAL_DIM = 64
HGRN_CHUNK = 64

NSA_WIDTH = NSA_HEADS * HEAD_DIM
KV_WIDTH = NSA_KV_HEADS * HEAD_DIM
DIFF_WIDTH = DIFF_HEADS * HEAD_DIM
HGRN_WIDTH = HGRN_HEADS * HGRN_VAL_DIM
HGRN_KEY_WIDTH = HGRN_HEADS * HGRN_KEY_DIM
MIX_WIDTH = NSA_WIDTH + DIFF_WIDTH + HGRN_WIDTH
N_BIAS_HEADS = NSA_HEADS + DIFF_HEADS

IN_SIZES = (NSA_WIDTH, KV_WIDTH, KV_WIDTH, KV_WIDTH, KV_WIDTH, KV_WIDTH, KV_WIDTH, NSA_HEADS * 3,
            DIFF_WIDTH, DIFF_WIDTH, DIFF_WIDTH,
            HGRN_KEY_WIDTH, HGRN_KEY_WIDTH, HGRN_WIDTH, HGRN_WIDTH)
IN_WIDTH = NSA_WIDTH + 6 * KV_WIDTH + 3 * NSA_HEADS + 3 * DIFF_WIDTH + 2 * HGRN_KEY_WIDTH + 2 * HGRN_WIDTH

kernel_name = 'hybrid_nsa_diff_hgrn2_macaron'


def rms_norm(x, gain):
    xf = x.astype(jnp.float32)
    xf = xf * lax.rsqrt(jnp.mean(xf * xf, axis=-1, keepdims=True) + EPS)
    return (xf * gain.astype(jnp.float32)).astype(x.dtype)


def swiglu(h, w_gate, w_up, w_down):
    return (jax.nn.silu(h @ w_gate) * (h @ w_up)) @ w_down


def t5_bucket(dist):
    n = jnp.maximum(dist, 0)
    max_exact = NUM_BUCKETS // 2
    nf = jnp.maximum(n, 1).astype(jnp.float32)
    far = max_exact + (jnp.log(nf / max_exact) / math.log(MAX_DISTANCE / max_exact)
                       * (NUM_BUCKETS - max_exact)).astype(jnp.int32)
    return jnp.where(n < max_exact, n, jnp.minimum(far, NUM_BUCKETS - 1))


def split_cols(z):
    bounds = np.cumsum(IN_SIZES)[:-1].tolist()
    return jnp.split(z, bounds, axis=-1)


def nsa_compress(kv, pos, w1, w2):
    b, t, hk, d = kv.shape
    halves = kv.reshape(b, t // CMP_STRIDE, CMP_STRIDE, hk, d)
    win = jnp.concatenate([halves[:, :-1], halves[:, 1:]], axis=2)
    win = win + pos[:, None, :]
    n = win.shape[1]
    flat = jnp.moveaxis(win, 3, 2).reshape(b, n, hk, CMP_BLOCK * d)
    return jax.nn.gelu(flat @ w1) @ w2


def nsa_attention(q, kc, vc, ks, vs, kw, vw, gates, bias_tab):
    b, t = q.shape[:2]
    n_cmp = kc.shape[1]
    n_slc = t // SLC_BLOCK
    n_sel = min(N_SELECT, n_slc)
    scale = HEAD_DIM ** -0.5
    qg = q.reshape(b, t, NSA_KV_HEADS, NSA_GROUP, HEAD_DIM)
    gg = gates.reshape(b, t, NSA_KV_HEADS, NSA_GROUP, 3)
    ks_blk = ks.reshape(b, n_slc, SLC_BLOCK, NSA_KV_HEADS, HEAD_DIM).transpose(0, 3, 1, 2, 4)
    vs_blk = vs.reshape(b, n_slc, SLC_BLOCK, NSA_KV_HEADS, HEAD_DIM).transpose(0, 3, 1, 2, 4)
    kw_pad = jnp.pad(kw, ((0, 0), (WINDOW, 0), (0, 0), (0, 0)))
    vw_pad = jnp.pad(vw, ((0, 0), (WINDOW, 0), (0, 0), (0, 0)))
    cmp_start = jnp.arange(n_cmp) * CMP_STRIDE
    slc_start = jnp.arange(n_slc) * SLC_BLOCK
    overlap = (jnp.maximum(jnp.minimum(cmp_start[:, None] + CMP_BLOCK, slc_start[None, :] + SLC_BLOCK)
                           - jnp.maximum(cmp_start[:, None], slc_start[None, :]), 0) / CMP_BLOCK).astype(jnp.float32)
    tab_sel = bias_tab.reshape(NUM_BUCKETS, NSA_KV_HEADS, NSA_GROUP).transpose(2, 1, 0).reshape(
        NSA_GROUP, NSA_KV_HEADS * NUM_BUCKETS)
    b_idx = jnp.arange(b)[:, None, None, None]
    g_idx = jnp.arange(NSA_KV_HEADS)[None, :, None, None]
    qoff = jnp.arange(QBLK)
    woff = jnp.arange(WINDOW + QBLK)
    tokoff = jnp.arange(SLC_BLOCK)

    def block(qb):
        q0 = qb * QBLK
        tpos = q0 + qoff
        qblk = lax.dynamic_slice_in_dim(qg, q0, QBLK, axis=1)
        s_c = jnp.einsum('bqgrd,bngd->bgrqn', qblk, kc).astype(jnp.float32) * scale
        valid_c = (cmp_start + CMP_BLOCK - 1)[None, :] <= tpos[:, None]
        p_c = jax.nn.softmax(jnp.where(valid_c, s_c, NEG), axis=-1) * jnp.any(valid_c, axis=-1)[:, None]
        o_c = jnp.einsum('bgrqn,bngd->bqgrd', p_c.astype(vc.dtype), vc)
        imp = jnp.einsum('bgrqn,nm->bgqm', p_c, overlap)
        blk = jnp.arange(n_slc)[None, :]
        cur = (tpos // SLC_BLOCK)[:, None]
        forced = (blk == 0) | (blk == cur) | (blk == cur - 1)
        imp = jnp.where(forced, FORCE, imp)
        imp = jnp.where(blk * SLC_BLOCK > tpos[:, None], NEG, imp)
        _, sel = lax.top_k(imp, n_sel)
        k_sel = ks_blk[b_idx, g_idx, sel].reshape(b, NSA_KV_HEADS, QBLK, n_sel * SLC_BLOCK, HEAD_DIM)
        v_sel = vs_blk[b_idx, g_idx, sel].reshape(b, NSA_KV_HEADS, QBLK, n_sel * SLC_BLOCK, HEAD_DIM)
        s_pos = (sel[..., None] * SLC_BLOCK + tokoff).reshape(b, NSA_KV_HEADS, QBLK, n_sel * SLC_BLOCK)
        dist = tpos[:, None] - s_pos
        bias = jnp.take(tab_sel, g_idx * NUM_BUCKETS + t5_bucket(dist), axis=1)
        s_s = jnp.einsum('bqgrd,bgqkd->bgrqk', qblk, k_sel).astype(jnp.float32) * scale + jnp.moveaxis(bias, 0, 2)
        p_s = jax.nn.softmax(jnp.where((dist >= 0)[:, :, None], s_s, NEG), axis=-1)
        o_s = jnp.einsum('bgrqk,bgqkd->bqgrd', p_s.astype(vs.dtype), v_sel)
        k_win = lax.dynamic_slice_in_dim(kw_pad, q0, WINDOW + QBLK, axis=1)
        v_win = lax.dynamic_slice_in_dim(vw_pad, q0, WINDOW + QBLK, axis=1)
        w_pos = q0 - WINDOW + woff
        dist_w = tpos[:, None] - w_pos[None, :]
        valid_w = (dist_w >= 0) & (dist_w < WINDOW) & (w_pos >= 0)[None, :]
        bias_w = bias_tab[t5_bucket(dist_w)].transpose(2, 0, 1).reshape(NSA_KV_HEADS, NSA_GROUP, QBLK, WINDOW + QBLK)
        s_w = jnp.einsum('bqgrd,bkgd->bgrqk', qblk, k_win).astype(jnp.float32) * scale + bias_w
        p_w = jax.nn.softmax(jnp.where(valid_w, s_w, NEG), axis=-1)
        o_w = jnp.einsum('bgrqk,bkgd->bqgrd', p_w.astype(vw.dtype), v_win)
        g = lax.dynamic_slice_in_dim(gg, q0, QBLK, axis=1)
        out = g[..., 0:1] * o_c + g[..., 1:2] * o_s + g[..., 2:3] * o_w
        return out.reshape(b, QBLK, NSA_WIDTH)

    out = lax.map(block, jnp.arange(t // QBLK))
    return jnp.moveaxis(out, 0, 1).reshape(b, t, NSA_WIDTH)


def diff_attention(q, k, v, lam, bias_tab):
    b, t = q.shape[:2]
    scale = DIFF_QK_DIM ** -0.5
    kpos = jnp.arange(t)
    qoff = jnp.arange(QBLK)

    def block(qb):
        q0 = qb * QBLK
        tpos = q0 + qoff
        qblk = lax.dynamic_slice_in_dim(q, q0, QBLK, axis=1)
        dist = tpos[:, None] - kpos[None, :]
        bias = bias_tab[t5_bucket(dist)].transpose(2, 0, 1)[None, :, None]
        s = jnp.einsum('bqhmd,bkhmd->bhmqk', qblk, k).astype(jnp.float32) * scale + bias
        p = jax.nn.softmax(jnp.where(dist >= 0, s, NEG), axis=-1)
        a = p[:, :, 0] - lam * p[:, :, 1]
        return jnp.einsum('bhqk,bkhd->bqhd', a.astype(v.dtype), v)

    out = lax.map(block, jnp.arange(t // QBLK))
    return jnp.moveaxis(out, 0, 1).reshape(b, t, DIFF_HEADS, HEAD_DIM)


def hgrn2_scan(q, k, v, log_f):
    b, t, h, dk = q.shape
    dv = v.shape[-1]
    nc = t // HGRN_CHUNK

    def chunks(a):
        return a.astype(jnp.float32).reshape(b, nc, HGRN_CHUNK, h, a.shape[-1]).transpose(1, 0, 3, 2, 4)

    causal = jnp.tril(jnp.ones((HGRN_CHUNK, HGRN_CHUNK), dtype=bool))

    def step(state, inp):
        qc, kc, vc, gc = inp
        cum = jnp.cumsum(gc, axis=2)
        inter = jnp.einsum('bhck,bhkv->bhcv', qc * jnp.exp(cum), state)
        decay = jnp.exp(jnp.where(causal[:, :, None], cum[:, :, :, None, :] - cum[:, :, None, :, :], -jnp.inf))
        scores = jnp.einsum('bhtk,bhsk,bhtsk->bhts', qc, kc, decay)
        intra = jnp.einsum('bhts,bhsv->bhtv', scores, vc)
        last = cum[:, :, -1:, :]
        new_state = jnp.exp(last[:, :, 0, :])[..., None] * state + jnp.einsum(
            'bhsk,bhsv->bhkv', kc * jnp.exp(last - cum), vc)
        return new_state, inter + intra

    s0 = jnp.zeros((b, h, dk, dv), jnp.float32)
    _, o = lax.scan(step, s0, (chunks(q), chunks(k), chunks(v), chunks(log_f)))
    return o.transpose(1, 0, 3, 2, 4).reshape(b, t, h, dv).astype(v.dtype)


def hybrid_mixer(h, layer, w_in, w_out, rel_bias, hgrn_lb, nsa_q_norm, nsa_k_norm, nsa_cmp_pos,
                 nsa_cmp_w1, nsa_cmp_w2, diff_q_norm, diff_k_norm, diff_lambda, diff_subln, hgrn_out_norm):
    b, t, _ = h.shape
    (nq, ck, cv, sk, sv, wk, wv, ng, dq, dk, dv, hq, hf, hi, hg) = split_cols(h @ w_in)

    def kvh(a):
        return a.reshape(b, t, NSA_KV_HEADS, HEAD_DIM)

    q_n = rms_norm(nq.reshape(b, t, NSA_HEADS, HEAD_DIM), nsa_q_norm)
    k_c = rms_norm(nsa_compress(kvh(ck), nsa_cmp_pos[0], nsa_cmp_w1[0], nsa_cmp_w2[0]), nsa_k_norm[0])
    v_c = nsa_compress(kvh(cv), nsa_cmp_pos[1], nsa_cmp_w1[1], nsa_cmp_w2[1])
    k_s = rms_norm(kvh(sk), nsa_k_norm[1])
    k_w = rms_norm(kvh(wk), nsa_k_norm[2])
    gates = jax.nn.sigmoid(ng.reshape(b, t, NSA_HEADS, 3))
    o_nsa = nsa_attention(q_n, k_c, v_c, k_s, kvh(sv), k_w, kvh(wv), gates, rel_bias[:, :NSA_HEADS])

    q_d = rms_norm(dq.reshape(b, t, DIFF_HEADS, 2, DIFF_QK_DIM), diff_q_norm)
    k_d = rms_norm(dk.reshape(b, t, DIFF_HEADS, 2, DIFF_QK_DIM), diff_k_norm)
    v_d = dv.reshape(b, t, DIFF_HEADS, HEAD_DIM)
    lam_init = 0.8 - 0.6 * math.exp(-0.3 * layer)
    lp = diff_lambda.astype(jnp.float32)
    lam = jnp.exp(jnp.sum(lp[0] * lp[1])) - jnp.exp(jnp.sum(lp[2] * lp[3])) + lam_init
    o_d = diff_attention(q_d, k_d, v_d, lam, rel_bias[:, NSA_HEADS:])
    o_d = rms_norm(o_d, diff_subln) * (1.0 - lam_init)

    fr = hf.reshape(b, t, HGRN_HEADS, HGRN_KEY_DIM).astype(jnp.float32)
    lb = hgrn_lb.reshape(HGRN_HEADS, HGRN_KEY_DIM)
    log_f = jnp.logaddexp(jnp.log(lb), jnp.log1p(-lb) + jax.nn.log_sigmoid(fr))
    i_gate = (1.0 - lb) * jax.nn.sigmoid(-fr)
    q_h = jax.nn.silu(hq.reshape(b, t, HGRN_HEADS, HGRN_KEY_DIM))
    o_h = hgrn2_scan(q_h, i_gate, hi.reshape(b, t, HGRN_HEADS, HGRN_VAL_DIM), log_f)
    o_h = rms_norm(o_h, hgrn_out_norm) * jax.nn.silu(hg.reshape(b, t, HGRN_HEADS, HGRN_VAL_DIM))

    mixed = jnp.concatenate([o_nsa, o_d.reshape(b, t, DIFF_WIDTH), o_h.reshape(b, t, HGRN_WIDTH)], axis=-1)
    return mixed @ w_out


def setup_inputs(seed: int = 0) -> dict:
    key = jax.random.key(seed)
    keys = iter(jax.random.split(key, 40))

    def normal(shape, scale):
        return jax.random.normal(next(keys), shape, jnp.float32) * scale

    def gain(shape):
        return 1.0 + 0.05 * jax.random.normal(next(keys), shape, jnp.float32)

    return {
        'x': normal((BATCH, SEQ, D_MODEL), 1.0),
        'p': normal((DEPTH, BATCH, SEQ, PLE_DIM), 1.0),
        'rel_bias': normal((NUM_BUCKETS, N_BIAS_HEADS), 0.2),
        'hgrn_lb_logits': normal((DEPTH, HGRN_KEY_WIDTH), 0.5),
        'ffn1_norm': gain((DEPTH, D_MODEL)),
        'ffn1_w_gate': normal((DEPTH, D_MODEL, D_FF), D_MODEL ** -0.5),
        'ffn1_w_up': normal((DEPTH, D_MODEL, D_FF), D_MODEL ** -0.5),
        'ffn1_w_down': normal((DEPTH, D_FF, D_MODEL), D_FF ** -0.5),
        'mix_norm': gain((DEPTH, D_MODEL)),
        'w_in': normal((DEPTH, D_MODEL, IN_WIDTH), D_MODEL ** -0.5),
        'w_out': normal((DEPTH, MIX_WIDTH, D_MODEL), MIX_WIDTH ** -0.5),
        'nsa_q_norm': gain((DEPTH, HEAD_DIM)),
        'nsa_k_norm': gain((DEPTH, 3, HEAD_DIM)),
        'nsa_cmp_pos': normal((DEPTH, 2, CMP_BLOCK, HEAD_DIM), 0.2),
        'nsa_cmp_w1': normal((DEPTH, 2, CMP_BLOCK * HEAD_DIM, CMP_HIDDEN), (CMP_BLOCK * HEAD_DIM) ** -0.5),
        'nsa_cmp_w2': normal((DEPTH, 2, CMP_HIDDEN, HEAD_DIM), CMP_HIDDEN ** -0.5),
        'diff_q_norm': gain((DEPTH, DIFF_QK_DIM)),
        'diff_k_norm': gain((DEPTH, DIFF_QK_DIM)),
        'diff_lambda': normal((DEPTH, 4, DIFF_QK_DIM), 0.1),
        'diff_subln': gain((DEPTH, HEAD_DIM)),
        'hgrn_out_norm': gain((DEPTH, HGRN_VAL_DIM)),
        'ffn2_norm': gain((DEPTH, D_MODEL)),
        'ffn2_w_gate': normal((DEPTH, D_MODEL, D_FF), D_MODEL ** -0.5),
        'ffn2_w_up': normal((DEPTH, D_MODEL, D_FF), D_MODEL ** -0.5),
        'ffn2_w_down': normal((DEPTH, D_FF, D_MODEL), D_FF ** -0.5),
        'ple_norm': gain((DEPTH, D_MODEL)),
        'ple_w_gate': normal((DEPTH, D_MODEL, D_MODEL), D_MODEL ** -0.5),
        'ple_w_proj': normal((DEPTH, PLE_DIM, D_MODEL), PLE_DIM ** -0.5),
        'ple_post_norm': gain((DEPTH, D_MODEL)),
    }


def reference(x, p, rel_bias, hgrn_lb_logits, ffn1_norm, ffn1_w_gate, ffn1_w_up, ffn1_w_down,
              mix_norm, w_in, w_out, nsa_q_norm, nsa_k_norm, nsa_cmp_pos, nsa_cmp_w1, nsa_cmp_w2,
              diff_q_norm, diff_k_norm, diff_lambda, diff_subln, hgrn_out_norm,
              ffn2_norm, ffn2_w_gate, ffn2_w_up, ffn2_w_down,
              ple_norm, ple_w_gate, ple_w_proj, ple_post_norm):
    lb_all = jnp.cumsum(jax.nn.softmax(hgrn_lb_logits.astype(jnp.float32), axis=0), axis=0)
    lb_all = lb_all - lb_all[0:1]
    for i in range(DEPTH):
        x = x + 0.5 * swiglu(rms_norm(x, ffn1_norm[i]), ffn1_w_gate[i], ffn1_w_up[i], ffn1_w_down[i])
        x = x + hybrid_mixer(rms_norm(x, mix_norm[i]), i, w_in[i], w_out[i], rel_bias, lb_all[i],
                             nsa_q_norm[i], nsa_k_norm[i], nsa_cmp_pos[i], nsa_cmp_w1[i], nsa_cmp_w2[i],
                             diff_q_norm[i], diff_k_norm[i], diff_lambda[i], diff_subln[i], hgrn_out_norm[i])
        x = x + 0.5 * swiglu(rms_norm(x, ffn2_norm[i]), ffn2_w_gate[i], ffn2_w_up[i], ffn2_w_down[i])
        gate = jax.nn.sigmoid(rms_norm(x, ple_norm[i]) @ ple_w_gate[i])
        x = x + gate * rms_norm(p[i] @ ple_w_proj[i], ple_post_norm[i])
    return x
```

```python
import functools
import math

import numpy as np
import jax
import jax.numpy as jnp
from jax import lax
from jax.experimental import pallas as pl
from jax.experimental.pallas import tpu as pltpu

F32 = jnp.float32
BF16 = jnp.bfloat16
HIGHEST = lax.Precision.HIGHEST

D_MODEL = 1024
D_FF = 2816
PLE_DIM = 256
HEAD_DIM = 64
NUM_BUCKETS = 32
MAX_DISTANCE = 128
NEG = -1e30
FORCE = 1e6
EPS = 1e-6
NSA_HEADS = 6
NSA_KV_HEADS = 2
NSA_GROUP = 3
CMP_BLOCK = 32
CMP_STRIDE = 16
CMP_HIDDEN = 256
SLC_BLOCK = 64
N_SELECT = 16
WINDOW = 512
DIFF_HEADS = 4
DIFF_QK_DIM = 32
HGRN_HEADS = 6
IN_SIZES = (384, 128, 128, 128, 128, 128, 128, 18, 256, 256, 256, 384, 384, 384, 384)

LANES = 128
VMEM_LIMIT = 56 * 1024 * 1024
TOKEN_TILE = 512
FF_CHUNK = 256
QT = 128
HCHUNK = 128


def _cparams(*sem):
    return pltpu.CompilerParams(dimension_semantics=sem, vmem_limit_bytes=VMEM_LIMIT)


def _const_spec(shape):
    nd = len(shape)
    return pl.BlockSpec(shape, lambda *_: (0,) * nd, pipeline_mode=pl.Buffered(1))


def _rms(x, gain):
    ms = jnp.mean(x * x, axis=-1, keepdims=True)
    return x * lax.rsqrt(ms + EPS) * gain


def _dot(a, b):
    return jnp.dot(a, b, preferred_element_type=F32)


def _dot_nt(a, b):
    return lax.dot_general(a, b, (((1,), (1,)), ((), ())), preferred_element_type=F32)


def _dot_tn(a, b):
    return lax.dot_general(a, b, (((0,), (0,)), ((), ())), preferred_element_type=F32)


def _dot_exact(a, b):
    return jnp.dot(a, b, preferred_element_type=F32, precision=HIGHEST)


def _ffn_body(x_ref, g_ref, wg_ref, wu_ref, wd_ref, o_ref):
    x = x_ref[...]
    h = _rms(x, g_ref[...]).astype(BF16)
    acc = jnp.zeros_like(x)
    for c in range(D_FF // FF_CHUNK):
        sl = slice(c * FF_CHUNK, (c + 1) * FF_CHUNK)
        a = _dot(h, wg_ref[:, sl])
        u = _dot(h, wu_ref[:, sl])
        act = (a * jax.nn.sigmoid(a) * u).astype(BF16)
        acc = acc + _dot(act, wd_ref[sl, :])
    o_ref[...] = x + 0.5 * acc


def _ffn(x, gain, wg, wu, wd):
    n = x.shape[0]
    tm = min(TOKEN_TILE, n)
    return pl.pallas_call(
        _ffn_body,
        grid=(n // tm,),
        in_specs=[pl.BlockSpec((tm, D_MODEL), lambda i: (i, 0)),
                  _const_spec((1, D_MODEL)),
                  _const_spec((D_MODEL, D_FF)),
                  _const_spec((D_MODEL, D_FF)),
                  _const_spec((D_FF, D_MODEL))],
        out_specs=pl.BlockSpec((tm, D_MODEL), lambda i: (i, 0)),
        out_shape=jax.ShapeDtypeStruct((n, D_MODEL), F32),
        compiler_params=_cparams("parallel"),
        name="ffn",
    )(x, gain, wg, wu, wd)


_C_NQ, _C_CK, _C_CV, _C_SK, _C_SV, _C_WK, _C_WV = 0, 384, 512, 640, 768, 896, 1024
_C_DQ, _C_DK, _C_DV = 1152, 1408, 1664
_C_HQ, _C_HF, _C_HI, _C_HG = 1920, 2304, 2688, 3072
_C_NG = 3456
IN_PERM_WIDTH = 3584


def _group_mean_sq(z, group):
    r = lax.broadcasted_iota(jnp.int32, (LANES, LANES), 0) // group
    c = lax.broadcasted_iota(jnp.int32, (LANES, LANES), 1) // group
    bd = jnp.where(r == c, 1.0 / group, 0.0).astype(F32)
    zz = z * z
    parts = [_dot_exact(zz[:, j * LANES:(j + 1) * LANES], bd) for j in range(z.shape[1] // LANES)]
    return parts[0] if len(parts) == 1 else jnp.concatenate(parts, axis=1)


def _group_norm(z, gain, group):
    return z * lax.rsqrt(_group_mean_sq(z, group) + EPS) * gain


def _inproj_body(seq_len, x_ref, g_ref, w_ref, qg_ref, skg_ref, wkg_ref, dqg_ref, dkg_ref,
                 q6_ref, ckv_ref, ska_ref, sv_ref, wk_ref, wv_ref, gt_ref,
                 dq_ref, dk_ref, dv_ref, hq_ref, hf_ref, hi_ref, hg_ref):
    tm = x_ref.shape[0]
    h = _rms(x_ref[...], g_ref[...]).astype(BF16)

    def proj(c0, width):
        return _dot(h, w_ref[:, c0:c0 + width])

    qn = _group_norm(proj(_C_NQ, 384), qg_ref[...], HEAD_DIM).astype(BF16)
    for hh in range(NSA_HEADS):
        q6_ref[hh] = qn[:, hh * 64:(hh + 1) * 64]

    ckv = proj(_C_CK, 256)
    for s in range(4):
        ckv_ref[s] = ckv[:, s * 64:(s + 1) * 64]

    skn = _group_norm(proj(_C_SK, 128), skg_ref[...], HEAD_DIM).astype(BF16)
    row = lax.broadcasted_iota(jnp.int32, (tm, LANES), 0) + pl.program_id(0) * tm
    blk = (row % seq_len) // SLC_BLOCK
    col = lax.broadcasted_iota(jnp.int32, (tm, LANES), 1)
    onehot = jnp.where(blk == col, 1.0, 0.0).astype(BF16)
    for g in range(2):
        ska_ref[g] = jnp.concatenate([onehot, skn[:, g * 64:(g + 1) * 64]], axis=1)
    sv = proj(_C_SV, 128).astype(BF16)
    wkn = _group_norm(proj(_C_WK, 128), wkg_ref[...], HEAD_DIM).astype(BF16)
    wv = proj(_C_WV, 128).astype(BF16)
    for g in range(2):
        sv_ref[g] = sv[:, g * 64:(g + 1) * 64]
        wk_ref[g] = wkn[:, g * 64:(g + 1) * 64]
        wv_ref[g] = wv[:, g * 64:(g + 1) * 64]

    gates = jax.nn.sigmoid(proj(_C_NG, 128))
    for g in range(2):
        gt_ref[g] = gates[:, g * 9:(g + 1) * 9]

    dq_ref[...] = _group_norm(proj(_C_DQ, 256), dqg_ref[...], DIFF_QK_DIM).astype(BF16)
    dk_ref[...] = _group_norm(proj(_C_DK, 256), dkg_ref[...], DIFF_QK_DIM).astype(BF16)
    dv_ref[...] = proj(_C_DV, 256).astype(BF16)

    hq_ref[...] = proj(_C_HQ, 384)
    hf_ref[...] = proj(_C_HF, 384)
    hi_ref[...] = proj(_C_HI, 384)
    hg_ref[...] = proj(_C_HG, 384)


def _inproj(x, seq_len, gain, w, qg, skg, wkg, dqg, dkg):
    n = x.shape[0]
    tm = min(TOKEN_TILE, seq_len)

    def rows(width, dtype):
        return (pl.BlockSpec((tm, width), lambda i: (i, 0)), jax.ShapeDtypeStruct((n, width), dtype))

    def heads(nh, width, dtype):
        return (pl.BlockSpec((nh, tm, width), lambda i: (0, i, 0)),
                jax.ShapeDtypeStruct((nh, n, width), dtype))

    outs = [heads(6, 64, BF16), heads(4, 64, F32), heads(2, 192, BF16), heads(2, 64, BF16),
            heads(2, 64, BF16), heads(2, 64, BF16), heads(2, 9, F32),
            rows(256, BF16), rows(256, BF16), rows(256, BF16),
            rows(384, F32), rows(384, F32), rows(384, F32), rows(384, F32)]
    return pl.pallas_call(
        functools.partial(_inproj_body, seq_len),
        grid=(n // tm,),
        in_specs=[pl.BlockSpec((tm, D_MODEL), lambda i: (i, 0)),
                  _const_spec((1, D_MODEL)),
                  _const_spec((D_MODEL, IN_PERM_WIDTH)),
                  _const_spec((1, 384)), _const_spec((1, 128)), _const_spec((1, 128)),
                  _const_spec((1, 256)), _const_spec((1, 256))],
        out_specs=[o[0] for o in outs],
        out_shape=[o[1] for o in outs],
        compiler_params=_cparams("parallel"),
        name="in_proj",
    )(x, gain, w, qg, skg, wkg, dqg, dkg)


def _gelu_tanh(x):
    return 0.5 * x * (1.0 + jnp.tanh(math.sqrt(2.0 / math.pi) * (x + 0.044715 * (x * x * x))))


def _compress_body(h_ref, pos_ref, w1_ref, w2_ref, kg_ref, o_ref):
    nh = h_ref.shape[0]
    ab = _dot(h_ref[...].astype(BF16), w1_ref[...])
    pt = _dot(pos_ref[0].astype(BF16), w1_ref[...])
    pb = _dot(pos_ref[1].astype(BF16), w1_ref[...])
    bias = pt[0:1, 0:CMP_HIDDEN] + pb[0:1, CMP_HIDDEN:2 * CMP_HIDDEN]
    nxt = pltpu.roll(ab[:, CMP_HIDDEN:2 * CMP_HIDDEN], nh - 1, 0)
    hid = _gelu_tanh(ab[:, 0:CMP_HIDDEN] + nxt + bias).astype(BF16)
    out = _dot(hid, w2_ref[...])
    is_key = pl.program_id(0) < 2
    o_ref[...] = jnp.where(is_key, _rms(out, kg_ref[...]), out).astype(o_ref.dtype)


def _compress(ckv_halves, pos, w1cat, w2, kgain):
    _, b, nh, _ = ckv_halves.shape
    return pl.pallas_call(
        _compress_body,
        grid=(4, b),
        in_specs=[pl.BlockSpec((None, None, nh, 1024), lambda s, i: (s, i, 0, 0)),
                  pl.BlockSpec((None, 2, 8, 1024), lambda s, i: (s // 2, 0, 0, 0)),
                  pl.BlockSpec((None, 1024, 2 * CMP_HIDDEN), lambda s, i: (s // 2, 0, 0)),
                  pl.BlockSpec((None, CMP_HIDDEN, HEAD_DIM), lambda s, i: (s // 2, 0, 0)),
                  pl.BlockSpec((1, HEAD_DIM), lambda s, i: (0, 0))],
        out_specs=pl.BlockSpec((None, None, nh, HEAD_DIM), lambda s, i: (s, i, 0, 0)),
        out_shape=jax.ShapeDtypeStruct((4, b, nh, HEAD_DIM), BF16),
        compiler_params=_cparams("parallel", "parallel"),
        name="compress",
    )(ckv_halves, pos, w1cat, w2, kgain)


def _flash_sweep(lhs, k_ref, v_ref, bias_ref, lo, hi, qi, kind_of):
    m_rows = lhs.shape[0]
    vd = v_ref.shape[-1]

    def body(kv, carry):
        m, l, acc = carry
        off = pl.multiple_of(kv * QT, QT)
        s = _dot_nt(lhs, k_ref[pl.ds(off, QT), :]) + bias_ref[kind_of(qi - kv)]
        m_new = jnp.maximum(m, jnp.max(s, axis=-1, keepdims=True))
        alpha = jnp.exp(m - m_new)
        p = jnp.exp(s - m_new)
        l = alpha * l + jnp.sum(p, axis=-1, keepdims=True)
        acc = alpha * acc + _dot(p.astype(BF16), v_ref[pl.ds(off, QT), :])
        return m_new, l, acc

    init = (jnp.full((m_rows, 1), NEG, F32), jnp.zeros((m_rows, 1), F32),
            jnp.zeros((m_rows, vd), F32))
    _, l, acc = lax.fori_loop(lo, hi, body, init)
    return acc, l


def _nsa_body(q_ref, kc_ref, vc_ref, ska_ref, sv_ref, wk_ref, wv_ref, gt_ref, bias_ref, ovl_ref,
              o_ref):
    qi = pl.program_id(2)
    q0 = qi * QT
    q3 = q_ref[...].reshape(NSA_GROUP * QT, HEAD_DIM)
    ncmp = kc_ref.shape[0]

    s_c = _dot_nt(q3, kc_ref[...])
    tpos = q0 + lax.broadcasted_iota(jnp.int32, s_c.shape, 0) % QT
    cend = lax.broadcasted_iota(jnp.int32, s_c.shape, 1) * CMP_STRIDE + (CMP_BLOCK - 1)
    s_c = jnp.where(cend <= tpos, s_c, NEG)
    p_c = jnp.exp(s_c - jnp.max(s_c, axis=-1, keepdims=True))
    p_c = p_c / jnp.sum(p_c, axis=-1, keepdims=True)
    p_c = jnp.where(tpos >= CMP_BLOCK - 1, p_c, 0.0)
    o_c = _dot(p_c.astype(BF16), vc_ref[...])

    p_sum = p_c[0:QT] + p_c[QT:2 * QT] + p_c[2 * QT:3 * QT]
    imp_t = _dot_exact(p_sum, ovl_ref[...]).T
    nslc = imp_t.shape[0]
    blk = lax.broadcasted_iota(jnp.int32, imp_t.shape, 0)
    qpos = q0 + lax.broadcasted_iota(jnp.int32, imp_t.shape, 1)
    cur = qpos // SLC_BLOCK
    forced = (blk == 0) | (blk == cur) | (blk == cur - 1)
    imp_t = jnp.where(forced, FORCE, imp_t)
    imp_t = jnp.where(blk * SLC_BLOCK > qpos, NEG, imp_t)

    def pick_one(_, carry):
        work, sel = carry
        top = jnp.max(work, axis=0, keepdims=True)
        first = jnp.min(jnp.where(work == top, blk, nslc), axis=0, keepdims=True)
        pick = blk == first
        return jnp.where(pick, -jnp.inf, work), jnp.where(pick, 1.0, sel)

    _, sel_t = lax.fori_loop(0, N_SELECT, pick_one, (imp_t, jnp.zeros_like(imp_t)))
    sel_neg = jnp.where(sel_t.T > 0.5, 0.0, NEG).astype(BF16)

    lhs_s = jnp.concatenate([jnp.concatenate([sel_neg] * NSA_GROUP, axis=0), q3], axis=1)
    acc_s, l_s = _flash_sweep(lhs_s, ska_ref, sv_ref, bias_ref, 0, qi + 1, qi,
                              lambda rel: jnp.minimum(rel, 2))
    acc_w, l_w = _flash_sweep(q3, wk_ref, wv_ref, bias_ref, jnp.maximum(qi - WINDOW // QT, 0),
                              qi + 1, qi,
                              lambda rel: jnp.where(rel == WINDOW // QT, 3, jnp.minimum(rel, 2)))
    o_s = acc_s / l_s
    o_w = acc_w / l_w

    gt = gt_ref[...]
    outs = []
    for r in range(NSA_GROUP):
        rs = slice(r * QT, (r + 1) * QT)
        outs.append(gt[:, 3 * r:3 * r + 1] * o_c[rs] + gt[:, 3 * r + 1:3 * r + 2] * o_s[rs]
                    + gt[:, 3 * r + 2:3 * r + 3] * o_w[rs])
    o_ref[...] = jnp.concatenate(outs, axis=1).astype(o_ref.dtype)


def _nsa(q6, kc, vc, ska, sv, wk, wv, gates, bias, overlap, batch, seq_len):
    n = batch * seq_len
    nq = seq_len // QT
    ncmp = kc.shape[2]
    nslc = overlap.shape[1]

    def seq_spec(width):
        return pl.BlockSpec((None, seq_len, width), lambda b, g, i: (g, b, 0))

    return pl.pallas_call(
        _nsa_body,
        grid=(batch, 2, nq),
        in_specs=[pl.BlockSpec((NSA_GROUP, QT, HEAD_DIM), lambda b, g, i: (g, b * nq + i, 0)),
                  pl.BlockSpec((None, None, ncmp, HEAD_DIM), lambda b, g, i: (g, b, 0, 0)),
                  pl.BlockSpec((None, None, ncmp, HEAD_DIM), lambda b, g, i: (g, b, 0, 0)),
                  seq_spec(LANES + HEAD_DIM), seq_spec(HEAD_DIM), seq_spec(HEAD_DIM),
                  seq_spec(HEAD_DIM),
                  pl.BlockSpec((None, QT, 9), lambda b, g, i: (g, b * nq + i, 0)),
                  pl.BlockSpec((None, 4, NSA_GROUP * QT, QT), lambda b, g, i: (g, 0, 0, 0)),
                  pl.BlockSpec((ncmp, nslc), lambda b, g, i: (0, 0))],
        out_specs=pl.BlockSpec((None, QT, NSA_GROUP * HEAD_DIM), lambda b, g, i: (g, b * nq + i, 0)),
        out_shape=jax.ShapeDtypeStruct((2, n, NSA_GROUP * HEAD_DIM), BF16),
        compiler_params=_cparams("parallel", "parallel", "arbitrary"),
        name="nsa",
    )(q6, kc, vc, ska, sv, wk, wv, gates, bias, overlap)


def _diff_body(lam_init, q_ref, k_ref, v_ref, bias_ref, lam_ref, sg_ref, o_ref):
    qi = pl.program_id(2)
    q = q_ref[...]
    lane = lax.broadcasted_iota(jnp.int32, q.shape, 1)
    zero = jnp.zeros_like(q)
    lhs = jnp.concatenate([jnp.where(lane // DIFF_QK_DIM == j, q, zero) for j in range(4)], axis=0)
    acc, l = _flash_sweep(lhs, k_ref, v_ref, bias_ref, 0, qi + 1, qi,
                          lambda rel: jnp.minimum(rel, 2))
    o = acc / l
    lp = lam_ref[...]
    lam = (jnp.exp(jnp.sum(lp[0:1] * lp[1:2], axis=-1, keepdims=True))
           - jnp.exp(jnp.sum(lp[2:3] * lp[3:4], axis=-1, keepdims=True)) + lam_init)
    lo_half = lane < HEAD_DIM
    d_even = o[0:QT] - lam * o[QT:2 * QT]
    d_odd = o[2 * QT:3 * QT] - lam * o[3 * QT:4 * QT]
    d = jnp.where(lo_half, d_even, d_odd)
    d2 = d * d
    ms_lo = jnp.sum(jnp.where(lo_half, d2, 0.0), axis=-1, keepdims=True)
    ms_hi = jnp.sum(jnp.where(lo_half, 0.0, d2), axis=-1, keepdims=True)
    ms = jnp.where(lo_half, ms_lo, ms_hi) * (1.0 / HEAD_DIM)
    out = d * lax.rsqrt(ms + EPS) * sg_ref[...] * (1.0 - lam_init)
    o_ref[...] = out.astype(o_ref.dtype)


def _diff(dq, dk, dv, bias, lam_p, subln, lam_init, batch, seq_len):
    n = batch * seq_len
    nq = seq_len // QT
    return pl.pallas_call(
        functools.partial(_diff_body, lam_init),
        grid=(batch, 2, nq),
        in_specs=[pl.BlockSpec((QT, LANES), lambda b, hp, i: (b * nq + i, hp)),
                  pl.BlockSpec((seq_len, LANES), lambda b, hp, i: (b, hp)),
                  pl.BlockSpec((seq_len, LANES), lambda b, hp, i: (b, hp)),
                  pl.BlockSpec((None, 3, 4 * QT, QT), lambda b, hp, i: (hp, 0, 0, 0)),
                  pl.BlockSpec((4, DIFF_QK_DIM), lambda b, hp, i: (0, 0)),
                  pl.BlockSpec((1, LANES), lambda b, hp, i: (0, 0))],
        out_specs=pl.BlockSpec((QT, LANES), lambda b, hp, i: (b * nq + i, hp)),
        out_shape=jax.ShapeDtypeStruct((n, 2 * LANES), BF16),
        compiler_params=_cparams("parallel", "parallel", "arbitrary"),
        name="diff",
    )(dq, dk, dv, bias, lam_p, subln)


def _hgrn_level_masks():
    t = np.arange(HCHUNK)[:, None]
    j = np.arange(HCHUNK)[None, :]
    mats = [(j <= t)]
    for lvl in range(1, 8):
        bs = (2 * HCHUNK) >> lvl
        mid = (t // bs) * bs + bs // 2
        mats.append(j < mid)
    return np.concatenate(mats, axis=0).astype(np.float32)


def _hgrn_body(hq_ref, hf_ref, hi_ref, hg_ref, lb_ref, gain_ref, mask_ref, o_ref, state_ref):
    @pl.when(pl.program_id(1) == 0)
    def _():
        state_ref[...] = jnp.zeros_like(state_ref)

    fr = hf_ref[...]
    lb = lb_ref[...]
    log_sig = jnp.minimum(fr, 0.0) - jnp.log1p(jnp.exp(-jnp.abs(fr)))
    a = jnp.log(lb)
    b = jnp.log1p(-lb) + log_sig
    log_f = jnp.maximum(a, b) + jnp.log1p(jnp.exp(-jnp.abs(a - b)))
    kk = (1.0 - lb) * jax.nn.sigmoid(-fr)
    hq = hq_ref[...]
    qq = hq * jax.nn.sigmoid(hq)
    vv = hi_ref[...].astype(BF16)

    sums = _dot_exact(mask_ref[...], log_f)
    cum = sums[0:HCHUNK]
    last = cum[HCHUNK - 1:HCHUNK]
    q_in = (qq * jnp.exp(cum)).astype(BF16)
    k_out = (kk * jnp.exp(last - cum)).astype(BF16)
    decay = jnp.exp(last)

    row = lax.broadcasted_iota(jnp.int32, (HCHUNK, 3 * LANES), 0)
    trow = lax.broadcasted_iota(jnp.int32, (2 * HCHUNK, HCHUNK), 0) % HCHUNK
    scol = lax.broadcasted_iota(jnp.int32, (2 * HCHUNK, HCHUNK), 1)
    lane = lax.broadcasted_iota(jnp.int32, (HCHUNK, LANES), 1)
    lo_half = lane < HEAD_DIM
    sq_r = lax.broadcasted_iota(jnp.int32, (LANES, LANES), 0) // HEAD_DIM
    sq_c = lax.broadcasted_iota(jnp.int32, (LANES, LANES), 1) // HEAD_DIM
    same_head = sq_r == sq_c

    def split_heads(x):
        z = jnp.zeros_like(x)
        return jnp.concatenate([jnp.where(lo_half, x, z), jnp.where(lo_half, z, x)], axis=0)

    q_lv, k_lv = [qq.astype(BF16)], [kk.astype(BF16)]
    for lvl in range(1, 8):
        bs = (2 * HCHUNK) >> lvl
        ref = sums[lvl * HCHUNK:(lvl + 1) * HCHUNK]
        upper = (row % bs) >= bs // 2
        q_lv.append((qq * jnp.exp(jnp.where(upper, cum - ref, -jnp.inf))).astype(BF16))
        k_lv.append((kk * jnp.exp(jnp.where(upper, -jnp.inf, ref - cum))).astype(BF16))

    outs = []
    for pr in range(HGRN_HEADS // 2):
        ls = slice(pr * LANES, (pr + 1) * LANES)
        scores = jnp.where(trow == scol, _dot_nt(split_heads(q_lv[0][:, ls]), k_lv[0][:, ls]), 0.0)
        for lvl in range(1, 8):
            bs = (2 * HCHUNK) >> lvl
            s_l = _dot_nt(split_heads(q_lv[lvl][:, ls]), k_lv[lvl][:, ls])
            scores = scores + jnp.where(trow // bs == scol // bs, s_l, 0.0)
        intra2 = _dot(scores.astype(BF16), vv[:, ls])
        intra = jnp.where(lo_half, intra2[0:HCHUNK], intra2[HCHUNK:2 * HCHUNK])
        st = state_ref[pr]
        inter = _dot_nt(q_in[:, ls], st.astype(BF16))
        upd = _dot_tn(vv[:, ls], k_out[:, ls])
        state_ref[pr] = st * decay[:, ls] + jnp.where(same_head, upd, 0.0)
        o = inter + intra
        o2 = o * o
        ms_lo = jnp.sum(jnp.where(lo_half, o2, 0.0), axis=-1, keepdims=True)
        ms_hi = jnp.sum(jnp.where(lo_half, 0.0, o2), axis=-1, keepdims=True)
        ms = jnp.where(lo_half, ms_lo, ms_hi) * (1.0 / HEAD_DIM)
        outs.append(o * lax.rsqrt(ms + EPS))
    hg = hg_ref[...]
    out = jnp.concatenate(outs, axis=1) * gain_ref[...] * (hg * jax.nn.sigmoid(hg))
    o_ref[...] = out.astype(o_ref.dtype)


def _hgrn(hq, hf, hi, hg, lb, gain, masks, batch, seq_len):
    n = batch * seq_len
    nc = seq_len // HCHUNK
    w = HGRN_HEADS * HEAD_DIM
    row_spec = pl.BlockSpec((HCHUNK, w), lambda b, c: (b * nc + c, 0))
    return pl.pallas_call(
        _hgrn_body,
        grid=(batch, nc),
        in_specs=[row_spec, row_spec, row_spec, row_spec,
                  pl.BlockSpec((1, w), lambda b, c: (0, 0)),
                  pl.BlockSpec((1, w), lambda b, c: (0, 0)),
                  pl.BlockSpec((8 * HCHUNK, HCHUNK), lambda b, c: (0, 0))],
        out_specs=row_spec,
        out_shape=jax.ShapeDtypeStruct((n, w), BF16),
        scratch_shapes=[pltpu.VMEM((HGRN_HEADS // 2, LANES, LANES), F32)],
        compiler_params=_cparams("parallel", "arbitrary"),
        name="hgrn",
    )(hq, hf, hi, hg, lb, gain, masks)


def _outproj_body(x_ref, on_ref, od_ref, oh_ref, wn_ref, wd_ref, wh_ref, o_ref):
    y = _dot(on_ref[0], wn_ref[0]) + _dot(on_ref[1], wn_ref[1])
    y = y + _dot(od_ref[...], wd_ref[...]) + _dot(oh_ref[...], wh_ref[...])
    o_ref[...] = x_ref[...] + y


def _outproj(x, o_nsa, o_d, o_h, wn, wd, wh):
    n = x.shape[0]
    tm = min(TOKEN_TILE, n)
    return pl.pallas_call(
        _outproj_body,
        grid=(n // tm,),
        in_specs=[pl.BlockSpec((tm, D_MODEL), lambda i: (i, 0)),
                  pl.BlockSpec((2, tm, 192), lambda i: (0, i, 0)),
                  pl.BlockSpec((tm, 256), lambda i: (i, 0)),
                  pl.BlockSpec((tm, 384), lambda i: (i, 0)),
                  _const_spec((2, 192, D_MODEL)), _const_spec((256, D_MODEL)),
                  _const_spec((384, D_MODEL))],
        out_specs=pl.BlockSpec((tm, D_MODEL), lambda i: (i, 0)),
        out_shape=jax.ShapeDtypeStruct((n, D_MODEL), F32),
        compiler_params=_cparams("parallel"),
        name="out_proj",
    )(x, o_nsa, o_d, o_h, wn, wd, wh)


def _ple_body(x_ref, p_ref, g_ref, wg_ref, wp_ref, pg_ref, o_ref):
    x = x_ref[...]
    gate = jax.nn.sigmoid(_dot(_rms(x, g_ref[...]).astype(BF16), wg_ref[...]))
    emb = _rms(_dot(p_ref[...].astype(BF16), wp_ref[...]), pg_ref[...])
    o_ref[...] = x + gate * emb


def _ple(x, p, gain, wg, wp, pgain):
    n = x.shape[0]
    tm = min(TOKEN_TILE, n)
    return pl.pallas_call(
        _ple_body,
        grid=(n // tm,),
        in_specs=[pl.BlockSpec((tm, D_MODEL), lambda i: (i, 0)),
                  pl.BlockSpec((tm, PLE_DIM), lambda i: (i, 0)),
                  _const_spec((1, D_MODEL)), _const_spec((D_MODEL, D_MODEL)),
                  _const_spec((PLE_DIM, D_MODEL)), _const_spec((1, D_MODEL))],
        out_specs=pl.BlockSpec((tm, D_MODEL), lambda i: (i, 0)),
        out_shape=jax.ShapeDtypeStruct((n, D_MODEL), F32),
        compiler_params=_cparams("parallel"),
        name="ple",
    )(x, p, gain, wg, wp, pgain)


def _t5_bucket(dist):
    n = jnp.maximum(dist, 0)
    max_exact = NUM_BUCKETS // 2
    nf = jnp.maximum(n, 1).astype(F32)
    far = max_exact + (jnp.log(nf / max_exact) / math.log(MAX_DISTANCE / max_exact)
                       * (NUM_BUCKETS - max_exact)).astype(jnp.int32)
    return jnp.where(n < max_exact, n, jnp.minimum(far, NUM_BUCKETS - 1))


def _bias_tiles(tab, with_window):
    tab = (tab - tab[NUM_BUCKETS - 1:NUM_BUCKETS]).T
    d = jnp.arange(QT)[:, None] - jnp.arange(QT)[None, :]
    diag = jnp.where(d >= 0, tab[:, _t5_bucket(d)], NEG)
    near = tab[:, _t5_bucket(d + QT)]
    kinds = [diag, near, jnp.zeros_like(near)]
    if with_window:
        edge = jnp.where(d < 0, 0.0, NEG).astype(F32)
        kinds.append(jnp.broadcast_to(edge, near.shape))
    return jnp.stack(kinds, axis=1).astype(F32)


def _overlap_matrix(ncmp_pad, nslc):
    cs = np.arange(ncmp_pad)[:, None] * CMP_STRIDE
    ss = np.arange(nslc)[None, :] * SLC_BLOCK
    ov = np.maximum(np.minimum(cs + CMP_BLOCK, ss + SLC_BLOCK) - np.maximum(cs, ss), 0) / CMP_BLOCK
    return jnp.asarray(ov, F32)


def _split_in_weights(w_in):
    bounds = np.cumsum(IN_SIZES)[:-1].tolist()
    (nq, ck, cv, sk, sv, wk, wv, ng, dq, dk, dv, hq, hf, hi, hg) = jnp.split(w_in, bounds, axis=-1)
    ng = jnp.pad(ng, ((0, 0), (0, LANES - ng.shape[1])))
    return jnp.concatenate([nq, ck, cv, sk, sv, wk, wv, dq, dk, dv, hq, hf, hi, hg, ng],
                           axis=1).astype(BF16)


def kernel(x, p, rel_bias, hgrn_lb_logits, ffn1_norm, ffn1_w_gate, ffn1_w_up, ffn1_w_down, mix_norm, w_in, w_out, nsa_q_norm, nsa_k_norm, nsa_cmp_pos, nsa_cmp_w1, nsa_cmp_w2, diff_q_norm, diff_k_norm, diff_lambda, diff_subln, hgrn_out_norm, ffn2_norm, ffn2_w_gate, ffn2_w_up, ffn2_w_down, ple_norm, ple_w_gate, ple_w_proj, ple_post_norm):
    batch, seq_len, _ = x.shape
    depth = p.shape[0]
    n = batch * seq_len
    assert seq_len % TOKEN_TILE == 0 and N_SELECT <= seq_len // SLC_BLOCK <= LANES
    nh = seq_len // CMP_STRIDE

    lb_all = jnp.cumsum(jax.nn.softmax(hgrn_lb_logits.astype(F32), axis=0), axis=0)
    lb_all = lb_all - lb_all[0:1]

    nsa_tiles = _bias_tiles(rel_bias[:, :NSA_HEADS], True)
    nsa_bias = nsa_tiles.reshape(2, NSA_GROUP, 4, QT, QT).transpose(0, 2, 1, 3, 4).reshape(
        2, 4, NSA_GROUP * QT, QT)
    diff_tiles = _bias_tiles(rel_bias[:, NSA_HEADS:], False)
    diff_bias = jnp.repeat(diff_tiles, 2, axis=0).reshape(2, 4, 3, QT, QT).transpose(
        0, 2, 1, 3, 4).reshape(2, 3, 4 * QT, QT)
    overlap = _overlap_matrix(nh, LANES)
    hgrn_masks = jnp.asarray(_hgrn_level_masks())

    xf = x.reshape(n, D_MODEL)
    pf = p.reshape(depth, n, PLE_DIM)
    row = lambda v: v.reshape(1, -1).astype(F32)
    for i in range(depth):
        xf = _ffn(xf, row(ffn1_norm[i]), ffn1_w_gate[i].astype(BF16), ffn1_w_up[i].astype(BF16),
                  ffn1_w_down[i].astype(BF16))

        qg = row(jnp.tile(nsa_q_norm[i], NSA_HEADS)) * (HEAD_DIM ** -0.5)
        dqg = row(jnp.tile(diff_q_norm[i], 2 * DIFF_HEADS)) * (DIFF_QK_DIM ** -0.5)
        (q6, ckv, ska, sv, wk, wv, gates, dq, dk, dv, hq, hf, hi, hg) = _inproj(
            xf, seq_len, row(mix_norm[i]), _split_in_weights(w_in[i]), qg,
            row(jnp.tile(nsa_k_norm[i, 1], 2)), row(jnp.tile(nsa_k_norm[i, 2], 2)), dqg,
            row(jnp.tile(diff_k_norm[i], 2 * DIFF_HEADS)))

        w1 = nsa_cmp_w1[i]
        w1cat = jnp.concatenate([w1[:, :1024], w1[:, 1024:]], axis=2).astype(BF16)
        pos = jnp.broadcast_to(nsa_cmp_pos[i].reshape(2, 2, 1, 1024), (2, 2, 8, 1024))
        cmp = _compress(ckv.reshape(4, batch, nh, 1024), pos, w1cat, nsa_cmp_w2[i].astype(BF16),
                        row(nsa_k_norm[i, 0]))
        o_nsa = _nsa(q6, cmp[0:2], cmp[2:4], ska, sv, wk, wv, gates, nsa_bias, overlap,
                     batch, seq_len)

        lam_init = 0.8 - 0.6 * math.exp(-0.3 * i)
        o_d = _diff(dq, dk, dv, diff_bias, diff_lambda[i].astype(F32),
                    row(jnp.tile(diff_subln[i], 2)), lam_init, batch, seq_len)

        o_h = _hgrn(hq, hf, hi, hg, row(lb_all[i]), row(jnp.tile(hgrn_out_norm[i], HGRN_HEADS)),
                    hgrn_masks, batch, seq_len)

        wo = w_out[i].astype(BF16)
        xf = _outproj(xf, o_nsa, o_d, o_h, wo[0:384].reshape(2, 192, D_MODEL), wo[384:640],
                      wo[640:1024])
        xf = _ffn(xf, row(ffn2_norm[i]), ffn2_w_gate[i].astype(BF16), ffn2_w_up[i].astype(BF16),
                  ffn2_w_down[i].astype(BF16))
        xf = _ple(xf, pf[i], row(ple_norm[i]), ple_w_gate[i].astype(BF16),
                  ple_w_proj[i].astype(BF16), row(ple_post_norm[i]))
    return xf.reshape(batch, seq_len, D_MODEL)
```

```python
import functools
import math

import numpy as np
import jax
import jax.numpy as jnp
from jax import lax
from jax.experimental import pallas as pl
from jax.experimental.pallas import tpu as pltpu

F32 = jnp.float32
BF16 = jnp.bfloat16
HIGHEST = lax.Precision.HIGHEST

D_MODEL = 1024
D_FF = 2816
PLE_DIM = 256
HEAD_DIM = 64
NUM_BUCKETS = 32
MAX_DISTANCE = 128
NEG = -1e30
FORCE = 1e6
EPS = 1e-6
NSA_HEADS = 6
NSA_KV_HEADS = 2
NSA_GROUP = 3
CMP_BLOCK = 32
CMP_STRIDE = 16
CMP_HIDDEN = 256
SLC_BLOCK = 64
N_SELECT = 16
WINDOW = 512
DIFF_HEADS = 4
DIFF_QK_DIM = 32
HGRN_HEADS = 6
IN_SIZES = (384, 128, 128, 128, 128, 128, 128, 18, 256, 256, 256, 384, 384, 384, 384)

LANES = 128
VMEM_LIMIT = 56 * 1024 * 1024
TOKEN_TILE = 512
FF_CHUNK = 256
AT = 256
ONES_ROWS = 16
HCHUNK = 128
LOG2E = math.log2(math.e)


def _cparams(*sem):
    return pltpu.CompilerParams(dimension_semantics=sem, vmem_limit_bytes=VMEM_LIMIT)


def _const_spec(shape):
    nd = len(shape)
    return pl.BlockSpec(shape, lambda *_: (0,) * nd, pipeline_mode=pl.Buffered(1))


def _rms(x, gain):
    ms = jnp.mean(x * x, axis=-1, keepdims=True)
    return x * lax.rsqrt(ms + EPS) * gain


def _dot(a, b):
    return jnp.dot(a, b, preferred_element_type=F32)


def _dot_nt(a, b):
    return lax.dot_general(a, b, (((1,), (1,)), ((), ())), preferred_element_type=F32)


def _dot_tn(a, b):
    return lax.dot_general(a, b, (((0,), (0,)), ((), ())), preferred_element_type=F32)


def _dot_exact(a, b):
    return jnp.dot(a, b, preferred_element_type=F32, precision=HIGHEST)


def _ffn_body(x_ref, g_ref, wg_ref, wu_ref, wd_ref, o_ref):
    x = x_ref[...]
    h = _rms(x, g_ref[...]).astype(BF16)
    acc = jnp.zeros_like(x)
    for c in range(D_FF // FF_CHUNK):
        sl = slice(c * FF_CHUNK, (c + 1) * FF_CHUNK)
        a = _dot(h, wg_ref[:, sl])
        u = _dot(h, wu_ref[:, sl])
        act = (a * jax.nn.sigmoid(a) * u).astype(BF16)
        acc = acc + _dot(act, wd_ref[sl, :])
    o_ref[...] = x + 0.5 * acc


def _ffn(x, gain, wg, wu, wd):
    n = x.shape[0]
    tm = min(TOKEN_TILE, n)
    return pl.pallas_call(
        _ffn_body,
        grid=(n // tm,),
        in_specs=[pl.BlockSpec((tm, D_MODEL), lambda i: (i, 0)),
                  _const_spec((1, D_MODEL)),
                  _const_spec((D_MODEL, D_FF)),
                  _const_spec((D_MODEL, D_FF)),
                  _const_spec((D_FF, D_MODEL))],
        out_specs=pl.BlockSpec((tm, D_MODEL), lambda i: (i, 0)),
        out_shape=jax.ShapeDtypeStruct((n, D_MODEL), F32),
        compiler_params=_cparams("parallel"),
        name="ffn",
    )(x, gain, wg, wu, wd)


_C_NQ, _C_CK, _C_CV, _C_SK, _C_SV, _C_WK, _C_WV = 0, 384, 512, 640, 768, 896, 1024
_C_DQ, _C_DK, _C_DV = 1152, 1408, 1664
_C_HQ, _C_HF, _C_HI, _C_HG = 1920, 2304, 2688, 3072
_C_NG = 3456
IN_PERM_WIDTH = 3584
GATE_ROWS = 16


def _group_mean_sq(z, group):
    r = lax.broadcasted_iota(jnp.int32, (LANES, LANES), 0) // group
    c = lax.broadcasted_iota(jnp.int32, (LANES, LANES), 1) // group
    bd = jnp.where(r == c, 1.0 / group, 0.0).astype(F32)
    zz = z * z
    parts = [_dot_exact(zz[:, j * LANES:(j + 1) * LANES], bd) for j in range(z.shape[1] // LANES)]
    return parts[0] if len(parts) == 1 else jnp.concatenate(parts, axis=1)


def _group_norm(z, gain, group):
    return z * lax.rsqrt(_group_mean_sq(z, group) + EPS) * gain


def _store_values_t(vt_ref, v, n_groups, width):
    tm = v.shape[0]
    vt = v.T.astype(BF16)
    ones = jnp.ones((ONES_ROWS, AT), BF16)
    for g in range(n_groups):
        for c in range(tm // AT):
            vt_ref[g, c] = jnp.concatenate(
                [vt[g * width:(g + 1) * width, c * AT:(c + 1) * AT], ones], axis=0)


def _inproj_body(seq_len, x_ref, g_ref, w_ref, qg_ref, skg_ref, wkg_ref, dqg_ref, dkg_ref,
                 qt_ref, ckv_ref, ska_ref, svt_ref, wk_ref, wvt_ref, gt_ref,
                 dqt_ref, dk_ref, dvt_ref, hq_ref, hf_ref, hi_ref, hg_ref):
    tm = x_ref.shape[0]
    h = _rms(x_ref[...], g_ref[...]).astype(BF16)

    def proj(c0, width):
        return _dot(h, w_ref[:, c0:c0 + width])

    qt_ref[...] = _group_norm(proj(_C_NQ, 384), qg_ref[...], HEAD_DIM).T.astype(BF16)

    ckv = proj(_C_CK, 256)
    for s in range(4):
        ckv_ref[s] = ckv[:, s * 64:(s + 1) * 64]

    skn = _group_norm(proj(_C_SK, 128), skg_ref[...], HEAD_DIM).astype(BF16)
    row = lax.broadcasted_iota(jnp.int32, (tm, LANES), 0) + pl.program_id(0) * tm
    blk = (row % seq_len) // SLC_BLOCK
    col = lax.broadcasted_iota(jnp.int32, (tm, LANES), 1)
    onehot = jnp.where(blk == col, 1.0, 0.0).astype(BF16)
    for g in range(2):
        ska_ref[g] = jnp.concatenate([onehot, skn[:, g * 64:(g + 1) * 64]], axis=1)
    _store_values_t(svt_ref, proj(_C_SV, 128), 2, HEAD_DIM)
    wkn = _group_norm(proj(_C_WK, 128), wkg_ref[...], HEAD_DIM).astype(BF16)
    for g in range(2):
        wk_ref[g] = wkn[:, g * 64:(g + 1) * 64]
    _store_values_t(wvt_ref, proj(_C_WV, 128), 2, HEAD_DIM)

    gates_t = jax.nn.sigmoid(proj(_C_NG, 128)).T
    for g in range(2):
        gt_ref[g] = gates_t[g * GATE_ROWS:(g + 1) * GATE_ROWS]

    dqt_ref[...] = _group_norm(proj(_C_DQ, 256), dqg_ref[...], DIFF_QK_DIM).T.astype(BF16)
    dk_ref[...] = _group_norm(proj(_C_DK, 256), dkg_ref[...], DIFF_QK_DIM).astype(BF16)
    _store_values_t(dvt_ref, proj(_C_DV, 256), 2, LANES)

    hq_ref[...] = proj(_C_HQ, 384)
    hf_ref[...] = proj(_C_HF, 384)
    hi_ref[...] = proj(_C_HI, 384)
    hg_ref[...] = proj(_C_HG, 384)


def _inproj(x, seq_len, gain, w, qg, skg, wkg, dqg, dkg):
    n = x.shape[0]
    tm = min(TOKEN_TILE, seq_len)

    def rows(width, dtype):
        return (pl.BlockSpec((tm, width), lambda i: (i, 0)), jax.ShapeDtypeStruct((n, width), dtype))

    def heads(nh, width, dtype):
        return (pl.BlockSpec((nh, tm, width), lambda i: (0, i, 0)),
                jax.ShapeDtypeStruct((nh, n, width), dtype))

    def cols(height, dtype):
        return (pl.BlockSpec((height, tm), lambda i: (0, i)), jax.ShapeDtypeStruct((height, n), dtype))

    def values_t(height):
        h = height + ONES_ROWS
        return (pl.BlockSpec((2, tm // AT, h, AT), lambda i: (0, i, 0, 0)),
                jax.ShapeDtypeStruct((2, n // AT, h, AT), BF16))

    gates_t = (pl.BlockSpec((2, GATE_ROWS, tm), lambda i: (0, 0, i)),
               jax.ShapeDtypeStruct((2, GATE_ROWS, n), F32))
    outs = [cols(384, BF16), heads(4, 64, F32), heads(2, 192, BF16), values_t(HEAD_DIM),
            heads(2, 64, BF16), values_t(HEAD_DIM), gates_t,
            cols(256, BF16), rows(256, BF16), values_t(LANES),
            rows(384, F32), rows(384, F32), rows(384, F32), rows(384, F32)]
    return pl.pallas_call(
        functools.partial(_inproj_body, seq_len),
        grid=(n // tm,),
        in_specs=[pl.BlockSpec((tm, D_MODEL), lambda i: (i, 0)),
                  _const_spec((1, D_MODEL)),
                  _const_spec((D_MODEL, IN_PERM_WIDTH)),
                  _const_spec((1, 384)), _const_spec((1, 128)), _const_spec((1, 128)),
                  _const_spec((1, 256)), _const_spec((1, 256))],
        out_specs=[o[0] for o in outs],
        out_shape=[o[1] for o in outs],
        compiler_params=_cparams("parallel"),
        name="in_proj",
    )(x, gain, w, qg, skg, wkg, dqg, dkg)


def _gelu_tanh(x):
    return 0.5 * x * (1.0 + jnp.tanh(math.sqrt(2.0 / math.pi) * (x + 0.044715 * (x * x * x))))


def _compress_body(h_ref, pos_ref, w1_ref, w2_ref, kg_ref, o_ref, ot_ref):
    nh = h_ref.shape[0]
    ab = _dot(h_ref[...].astype(BF16), w1_ref[...])
    pt = _dot(pos_ref[0].astype(BF16), w1_ref[...])
    pb = _dot(pos_ref[1].astype(BF16), w1_ref[...])
    bias = pt[0:1, 0:CMP_HIDDEN] + pb[0:1, CMP_HIDDEN:2 * CMP_HIDDEN]
    nxt = pltpu.roll(ab[:, CMP_HIDDEN:2 * CMP_HIDDEN], nh - 1, 0)
    hid = _gelu_tanh(ab[:, 0:CMP_HIDDEN] + nxt + bias).astype(BF16)
    out = _dot(hid, w2_ref[...])
    is_key = pl.program_id(0) < 2
    out = jnp.where(is_key, _rms(out, kg_ref[...]), out)
    o_ref[...] = out.astype(o_ref.dtype)
    out_t = jnp.concatenate([out, jnp.zeros_like(out)], axis=1).T
    ot_ref[...] = out_t[0:HEAD_DIM].astype(ot_ref.dtype)


def _compress(ckv_halves, pos, w1cat, w2, kgain):
    _, b, nh, _ = ckv_halves.shape
    return pl.pallas_call(
        _compress_body,
        grid=(4, b),
        in_specs=[pl.BlockSpec((None, None, nh, 1024), lambda s, i: (s, i, 0, 0)),
                  pl.BlockSpec((None, 2, 8, 1024), lambda s, i: (s // 2, 0, 0, 0)),
                  pl.BlockSpec((None, 1024, 2 * CMP_HIDDEN), lambda s, i: (s // 2, 0, 0)),
                  pl.BlockSpec((None, CMP_HIDDEN, HEAD_DIM), lambda s, i: (s // 2, 0, 0)),
                  pl.BlockSpec((1, HEAD_DIM), lambda s, i: (0, 0))],
        out_specs=[pl.BlockSpec((None, None, nh, HEAD_DIM), lambda s, i: (s, i, 0, 0)),
                   pl.BlockSpec((None, None, HEAD_DIM, nh), lambda s, i: (s, i, 0, 0))],
        out_shape=[jax.ShapeDtypeStruct((4, b, nh, HEAD_DIM), BF16),
                   jax.ShapeDtypeStruct((4, b, HEAD_DIM, nh), BF16)],
        compiler_params=_cparams("parallel", "parallel"),
        name="compress",
    )(ckv_halves, pos, w1cat, w2, kgain)


def _sweep(k_ref, rhs, vt_ref, bias_ref, kind_of, lo, hi, qi, m_ref, acc_ref, sa_ref, sb_ref):
    m_ref[...] = jnp.full(m_ref.shape, NEG, F32)
    acc_ref[...] = jnp.zeros(acc_ref.shape, F32)

    def produce(s_ref, j):
        off = pl.multiple_of(j * AT, AT)
        s_ref[...] = _dot(k_ref[pl.ds(off, AT), :], rhs)

    def consume(s_ref, j):
        vt = vt_ref[j]
        kind = kind_of(qi - j)
        for c in range(rhs.shape[1] // AT):
            cs = slice(c * AT, (c + 1) * AT)
            s = s_ref[:, cs] + bias_ref[kind, :, cs]
            m_old = m_ref[:, cs]
            m_new = jnp.maximum(m_old, jnp.max(s, axis=0, keepdims=True))
            p = jnp.exp2(s - m_new).astype(BF16)
            acc_ref[:, cs] = jnp.exp2(m_old - m_new) * acc_ref[:, cs] + _dot(vt, p)
            m_ref[:, cs] = m_new

    n = hi - lo
    produce(sa_ref, lo)

    def pair(t, c):
        j = lo + 2 * t
        produce(sb_ref, j + 1)
        consume(sa_ref, j)
        produce(sa_ref, jnp.minimum(j + 2, hi - 1))
        consume(sb_ref, j + 1)
        return c

    lax.fori_loop(0, n // 2, pair, 0)

    @pl.when(n % 2 == 1)
    def _():
        consume(sa_ref, hi - 1)


def _nsa_body(q_ref, kc_ref, vct_ref, ska_ref, svt_ref, wk_ref, wvt_ref, gt_ref, bias_ref, ovl_ref,
              o_ref, m_ref, acc_ref, sa_ref, sb_ref):
    qi = pl.program_id(2)
    q0 = qi * AT
    qblk = q_ref[...]
    q3t = jnp.concatenate([qblk[r * HEAD_DIM:(r + 1) * HEAD_DIM] for r in range(NSA_GROUP)],
                          axis=1)

    s_c = _dot(kc_ref[...], q3t)
    tpos = q0 + lax.broadcasted_iota(jnp.int32, s_c.shape, 1) % AT
    cend = lax.broadcasted_iota(jnp.int32, s_c.shape, 0) * CMP_STRIDE + (CMP_BLOCK - 1)
    s_c = jnp.where(cend <= tpos, s_c, NEG)
    p_c = jnp.exp2(s_c - jnp.max(s_c, axis=0, keepdims=True))
    p_c = p_c / jnp.sum(p_c, axis=0, keepdims=True)
    p_c = jnp.where(tpos >= CMP_BLOCK - 1, p_c, 0.0)
    o_ct = _dot(vct_ref[...], p_c.astype(BF16))

    p_sum = p_c[:, 0:AT] + p_c[:, AT:2 * AT] + p_c[:, 2 * AT:3 * AT]
    imp_t = _dot_exact(ovl_ref[...], p_sum)
    nslc = imp_t.shape[0]
    blk = lax.broadcasted_iota(jnp.int32, imp_t.shape, 0)
    qpos = q0 + lax.broadcasted_iota(jnp.int32, imp_t.shape, 1)
    cur = qpos // SLC_BLOCK
    forced = (blk == 0) | (blk == cur) | (blk == cur - 1)
    imp_t = jnp.where(forced, FORCE, imp_t)
    imp_t = jnp.where(blk * SLC_BLOCK > qpos, NEG, imp_t)

    def pick_one(_, carry):
        work, sel = carry
        top = jnp.max(work, axis=0, keepdims=True)
        first = jnp.min(jnp.where(work == top, blk, nslc), axis=0, keepdims=True)
        pick = blk == first
        return jnp.where(pick, -jnp.inf, work), jnp.where(pick, 1.0, sel)

    _, sel_t = lax.fori_loop(0, N_SELECT, pick_one, (imp_t, jnp.zeros_like(imp_t)))
    sel_neg = jnp.where(sel_t > 0.5, 0.0, NEG).astype(BF16)

    rhs_s = jnp.concatenate([jnp.concatenate([sel_neg] * NSA_GROUP, axis=1), q3t], axis=0)
    _sweep(ska_ref, rhs_s, svt_ref, bias_ref, lambda rel: jnp.minimum(rel, 2), 0, qi + 1, qi,
           m_ref, acc_ref, sa_ref, sb_ref)
    acc = acc_ref[...]
    o_st = acc[0:HEAD_DIM] / acc[HEAD_DIM:HEAD_DIM + 1]
    _sweep(wk_ref, q3t, wvt_ref, bias_ref, lambda rel: jnp.where(rel == WINDOW // AT, 3, rel),
           jnp.maximum(qi - WINDOW // AT, 0), qi + 1, qi, m_ref, acc_ref, sa_ref, sb_ref)
    acc = acc_ref[...]
    o_wt = acc[0:HEAD_DIM] / acc[HEAD_DIM:HEAD_DIM + 1]

    gt = gt_ref[...]
    rows = []
    for r in range(NSA_GROUP):
        cs = slice(r * AT, (r + 1) * AT)
        rows.append(gt[3 * r:3 * r + 1] * o_ct[:, cs] + gt[3 * r + 1:3 * r + 2] * o_st[:, cs]
                    + gt[3 * r + 2:3 * r + 3] * o_wt[:, cs])
    rows.append(jnp.zeros((AT - NSA_GROUP * HEAD_DIM, AT), F32))
    out = jnp.concatenate(rows, axis=0).T
    o_ref[...] = out[:, 0:NSA_GROUP * HEAD_DIM].astype(o_ref.dtype)


def _nsa(qt, cmp_k, cmp_vt, ska, svt, wk, wvt, gates_t, bias, overlap_t, batch, seq_len):
    n = batch * seq_len
    nq = seq_len // AT
    ncmp = cmp_k.shape[2]
    nslc = overlap_t.shape[0]
    cols = NSA_GROUP * AT
    vrows = HEAD_DIM + ONES_ROWS

    def seq_spec(width):
        return pl.BlockSpec((None, seq_len, width), lambda b, g, i: (g, b, 0))

    vt_spec = pl.BlockSpec((None, nq, vrows, AT), lambda b, g, i: (g, b, 0, 0))
    return pl.pallas_call(
        _nsa_body,
        grid=(batch, 2, nq),
        in_specs=[pl.BlockSpec((NSA_GROUP * HEAD_DIM, AT), lambda b, g, i: (g, b * nq + i)),
                  pl.BlockSpec((None, None, ncmp, HEAD_DIM), lambda b, g, i: (g, b, 0, 0)),
                  pl.BlockSpec((None, None, HEAD_DIM, ncmp), lambda b, g, i: (2 + g, b, 0, 0)),
                  seq_spec(LANES + HEAD_DIM), vt_spec, seq_spec(HEAD_DIM), vt_spec,
                  pl.BlockSpec((None, GATE_ROWS, AT), lambda b, g, i: (g, 0, b * nq + i)),
                  pl.BlockSpec((None, 4, AT, cols), lambda b, g, i: (g, 0, 0, 0)),
                  pl.BlockSpec((nslc, ncmp), lambda b, g, i: (0, 0))],
        out_specs=pl.BlockSpec((None, AT, NSA_GROUP * HEAD_DIM), lambda b, g, i: (g, b * nq + i, 0)),
        out_shape=jax.ShapeDtypeStruct((2, n, NSA_GROUP * HEAD_DIM), BF16),
        scratch_shapes=[pltpu.VMEM((1, cols), F32), pltpu.VMEM((vrows, cols), F32),
                        pltpu.VMEM((AT, cols), F32), pltpu.VMEM((AT, cols), F32)],
        compiler_params=_cparams("parallel", "parallel", "arbitrary"),
        name="nsa",
    )(qt, cmp_k, cmp_vt, ska, svt, wk, wvt, gates_t, bias, overlap_t)


def _diff_body(lam_init, q_ref, k_ref, vt_ref, bias_ref, lam_ref, sg_ref, o_ref, m_ref, acc_ref,
               sa_ref, sb_ref):
    qi = pl.program_id(2)
    qblk = q_ref[...]
    row = lax.broadcasted_iota(jnp.int32, qblk.shape, 0)
    zero = jnp.zeros_like(qblk)
    rhs = jnp.concatenate([jnp.where(row // DIFF_QK_DIM == j, qblk, zero) for j in range(4)], axis=1)
    _sweep(k_ref, rhs, vt_ref, bias_ref, lambda rel: jnp.minimum(rel, 2), 0, qi + 1, qi,
           m_ref, acc_ref, sa_ref, sb_ref)
    acc = acc_ref[...]
    o = acc[0:LANES] / acc[LANES:LANES + 1]
    lp = lam_ref[...]
    lam = (jnp.exp(jnp.sum(lp[0:1] * lp[1:2], axis=-1, keepdims=True))
           - jnp.exp(jnp.sum(lp[2:3] * lp[3:4], axis=-1, keepdims=True)) + lam_init)
    top = row < HEAD_DIM
    d_even = o[:, 0:AT] - lam * o[:, AT:2 * AT]
    d_odd = o[:, 2 * AT:3 * AT] - lam * o[:, 3 * AT:4 * AT]
    d = jnp.where(top, d_even, d_odd)
    d2 = d * d
    ms_top = jnp.sum(jnp.where(top, d2, 0.0), axis=0, keepdims=True)
    ms_bot = jnp.sum(jnp.where(top, 0.0, d2), axis=0, keepdims=True)
    ms = jnp.where(top, ms_top, ms_bot) * (1.0 / HEAD_DIM)
    out = (d * lax.rsqrt(ms + EPS)).T * sg_ref[...] * (1.0 - lam_init)
    o_ref[...] = out.astype(o_ref.dtype)


def _diff(dqt, dk, dvt, bias, lam_p, subln, lam_init, batch, seq_len):
    n = batch * seq_len
    nq = seq_len // AT
    cols = 4 * AT
    vrows = LANES + ONES_ROWS
    return pl.pallas_call(
        functools.partial(_diff_body, lam_init),
        grid=(batch, 2, nq),
        in_specs=[pl.BlockSpec((LANES, AT), lambda b, hp, i: (hp, b * nq + i)),
                  pl.BlockSpec((seq_len, LANES), lambda b, hp, i: (b, hp)),
                  pl.BlockSpec((None, nq, vrows, AT), lambda b, hp, i: (hp, b, 0, 0)),
                  pl.BlockSpec((None, 3, AT, cols), lambda b, hp, i: (hp, 0, 0, 0)),
                  pl.BlockSpec((4, DIFF_QK_DIM), lambda b, hp, i: (0, 0)),
                  pl.BlockSpec((1, LANES), lambda b, hp, i: (0, 0))],
        out_specs=pl.BlockSpec((AT, LANES), lambda b, hp, i: (b * nq + i, hp)),
        out_shape=jax.ShapeDtypeStruct((n, 2 * LANES), BF16),
        scratch_shapes=[pltpu.VMEM((1, cols), F32), pltpu.VMEM((vrows, cols), F32),
                        pltpu.VMEM((AT, cols), F32), pltpu.VMEM((AT, cols), F32)],
        compiler_params=_cparams("parallel", "parallel", "arbitrary"),
        name="diff",
    )(dqt, dk, dvt, bias, lam_p, subln)


def _hgrn_level_masks():
    t = np.arange(HCHUNK)[:, None]
    j = np.arange(HCHUNK)[None, :]
    mats = [(j <= t)]
    for lvl in range(1, 8):
        bs = (2 * HCHUNK) >> lvl
        mid = (t // bs) * bs + bs // 2
        mats.append(j < mid)
    return np.concatenate(mats, axis=0).astype(np.float32)


def _hgrn_body(hq_ref, hf_ref, hi_ref, hg_ref, lb_ref, gain_ref, mask_ref, o_ref, state_ref):
    @pl.when(pl.program_id(1) == 0)
    def _():
        state_ref[...] = jnp.zeros_like(state_ref)

    fr = hf_ref[...]
    lb = lb_ref[...]
    log_sig = jnp.minimum(fr, 0.0) - jnp.log1p(jnp.exp(-jnp.abs(fr)))
    a = jnp.log(lb)
    b = jnp.log1p(-lb) + log_sig
    log_f = jnp.maximum(a, b) + jnp.log1p(jnp.exp(-jnp.abs(a - b)))
    kk = (1.0 - lb) * jax.nn.sigmoid(-fr)
    hq = hq_ref[...]
    qq = hq * jax.nn.sigmoid(hq)
    vv = hi_ref[...].astype(BF16)

    sums = _dot_exact(mask_ref[...], log_f)
    cum = sums[0:HCHUNK]
    last = cum[HCHUNK - 1:HCHUNK]
    q_in = (qq * jnp.exp(cum)).astype(BF16)
    k_out = (kk * jnp.exp(last - cum)).astype(BF16)
    decay = jnp.exp(last)

    row = lax.broadcasted_iota(jnp.int32, (HCHUNK, 3 * LANES), 0)
    trow = lax.broadcasted_iota(jnp.int32, (2 * HCHUNK, HCHUNK), 0) % HCHUNK
    scol = lax.broadcasted_iota(jnp.int32, (2 * HCHUNK, HCHUNK), 1)
    lane = lax.broadcasted_iota(jnp.int32, (HCHUNK, LANES), 1)
    lo_half = lane < HEAD_DIM
    sq_r = lax.broadcasted_iota(jnp.int32, (LANES, LANES), 0) // HEAD_DIM
    sq_c = lax.broadcasted_iota(jnp.int32, (LANES, LANES), 1) // HEAD_DIM
    same_head = sq_r == sq_c

    def split_heads(x):
        z = jnp.zeros_like(x)
        return jnp.concatenate([jnp.where(lo_half, x, z), jnp.where(lo_half, z, x)], axis=0)

    q_lv, k_lv = [qq.astype(BF16)], [kk.astype(BF16)]
    for lvl in range(1, 8):
        bs = (2 * HCHUNK) >> lvl
        ref = sums[lvl * HCHUNK:(lvl + 1) * HCHUNK]
        upper = (row % bs) >= bs // 2
        q_lv.append((qq * jnp.exp(jnp.where(upper, cum - ref, -jnp.inf))).astype(BF16))
        k_lv.append((kk * jnp.exp(jnp.where(upper, -jnp.inf, ref - cum))).astype(BF16))

    outs = []
    for pr in range(HGRN_HEADS // 2):
        ls = slice(pr * LANES, (pr + 1) * LANES)
        scores = jnp.where(trow == scol, _dot_nt(split_heads(q_lv[0][:, ls]), k_lv[0][:, ls]), 0.0)
        for lvl in range(1, 8):
            bs = (2 * HCHUNK) >> lvl
            s_l = _dot_nt(split_heads(q_lv[lvl][:, ls]), k_lv[lvl][:, ls])
            scores = scores + jnp.where(trow // bs == scol // bs, s_l, 0.0)
        intra2 = _dot(scores.astype(BF16), vv[:, ls])
        intra = jnp.where(lo_half, intra2[0:HCHUNK], intra2[HCHUNK:2 * HCHUNK])
        st = state_ref[pr]
        inter = _dot_nt(q_in[:, ls], st.astype(BF16))
        upd = _dot_tn(vv[:, ls], k_out[:, ls])
        state_ref[pr] = st * decay[:, ls] + jnp.where(same_head, upd, 0.0)
        o = inter + intra
        o2 = o * o
        ms_lo = jnp.sum(jnp.where(lo_half, o2, 0.0), axis=-1, keepdims=True)
        ms_hi = jnp.sum(jnp.where(lo_half, 0.0, o2), axis=-1, keepdims=True)
        ms = jnp.where(lo_half, ms_lo, ms_hi) * (1.0 / HEAD_DIM)
        outs.append(o * lax.rsqrt(ms + EPS))
    hg = hg_ref[...]
    out = jnp.concatenate(outs, axis=1) * gain_ref[...] * (hg * jax.nn.sigmoid(hg))
    o_ref[...] = out.astype(o_ref.dtype)


def _hgrn(hq, hf, hi, hg, lb, gain, masks, batch, seq_len):
    n = batch * seq_len
    nc = seq_len // HCHUNK
    w = HGRN_HEADS * HEAD_DIM
    row_spec = pl.BlockSpec((HCHUNK, w), lambda b, c: (b * nc + c, 0))
    return pl.pallas_call(
        _hgrn_body,
        grid=(batch, nc),
        in_specs=[row_spec, row_spec, row_spec, row_spec,
                  pl.BlockSpec((1, w), lambda b, c: (0, 0)),
                  pl.BlockSpec((1, w), lambda b, c: (0, 0)),
                  pl.BlockSpec((8 * HCHUNK, HCHUNK), lambda b, c: (0, 0))],
        out_specs=row_spec,
        out_shape=jax.ShapeDtypeStruct((n, w), BF16),
        scratch_shapes=[pltpu.VMEM((HGRN_HEADS // 2, LANES, LANES), F32)],
        compiler_params=_cparams("parallel", "arbitrary"),
        name="hgrn",
    )(hq, hf, hi, hg, lb, gain, masks)


def _outproj_body(x_ref, on_ref, od_ref, oh_ref, wn_ref, wd_ref, wh_ref, o_ref):
    y = _dot(on_ref[0], wn_ref[0]) + _dot(on_ref[1], wn_ref[1])
    y = y + _dot(od_ref[...], wd_ref[...]) + _dot(oh_ref[...], wh_ref[...])
    o_ref[...] = x_ref[...] + y


def _outproj(x, o_nsa, o_d, o_h, wn, wd, wh):
    n = x.shape[0]
    tm = min(TOKEN_TILE, n)
    return pl.pallas_call(
        _outproj_body,
        grid=(n // tm,),
        in_specs=[pl.BlockSpec((tm, D_MODEL), lambda i: (i, 0)),
                  pl.BlockSpec((2, tm, 192), lambda i: (0, i, 0)),
                  pl.BlockSpec((tm, 256), lambda i: (i, 0)),
                  pl.BlockSpec((tm, 384), lambda i: (i, 0)),
                  _const_spec((2, 192, D_MODEL)), _const_spec((256, D_MODEL)),
                  _const_spec((384, D_MODEL))],
        out_specs=pl.BlockSpec((tm, D_MODEL), lambda i: (i, 0)),
        out_shape=jax.ShapeDtypeStruct((n, D_MODEL), F32),
        compiler_params=_cparams("parallel"),
        name="out_proj",
    )(x, o_nsa, o_d, o_h, wn, wd, wh)


def _ple_body(x_ref, p_ref, g_ref, wg_ref, wp_ref, pg_ref, o_ref):
    x = x_ref[...]
    gate = jax.nn.sigmoid(_dot(_rms(x, g_ref[...]).astype(BF16), wg_ref[...]))
    emb = _rms(_dot(p_ref[...].astype(BF16), wp_ref[...]), pg_ref[...])
    o_ref[...] = x + gate * emb


def _ple(x, p, gain, wg, wp, pgain):
    n = x.shape[0]
    tm = min(TOKEN_TILE, n)
    return pl.pallas_call(
        _ple_body,
        grid=(n // tm,),
        in_specs=[pl.BlockSpec((tm, D_MODEL), lambda i: (i, 0)),
                  pl.BlockSpec((tm, PLE_DIM), lambda i: (i, 0)),
                  _const_spec((1, D_MODEL)), _const_spec((D_MODEL, D_MODEL)),
                  _const_spec((PLE_DIM, D_MODEL)), _const_spec((1, D_MODEL))],
        out_specs=pl.BlockSpec((tm, D_MODEL), lambda i: (i, 0)),
        out_shape=jax.ShapeDtypeStruct((n, D_MODEL), F32),
        compiler_params=_cparams("parallel"),
        name="ple",
    )(x, p, gain, wg, wp, pgain)


def _t5_bucket(dist):
    n = jnp.maximum(dist, 0)
    max_exact = NUM_BUCKETS // 2
    nf = jnp.maximum(n, 1).astype(F32)
    far = max_exact + (jnp.log(nf / max_exact) / math.log(MAX_DISTANCE / max_exact)
                       * (NUM_BUCKETS - max_exact)).astype(jnp.int32)
    return jnp.where(n < max_exact, n, jnp.minimum(far, NUM_BUCKETS - 1))


def _bias_tiles(tab, with_window):
    tab = (tab - tab[NUM_BUCKETS - 1:NUM_BUCKETS]).T * LOG2E
    d = jnp.arange(AT)[None, :] - jnp.arange(AT)[:, None]
    diag = jnp.where(d >= 0, tab[:, _t5_bucket(d)], NEG)
    near = tab[:, _t5_bucket(d + AT)]
    kinds = [diag, near, jnp.zeros_like(near)]
    if with_window:
        edge = jnp.where(d < 0, 0.0, NEG).astype(F32)
        kinds.append(jnp.broadcast_to(edge, near.shape))
    return jnp.stack(kinds, axis=1).astype(F32)


def _overlap_matrix_t(ncmp_pad, nslc):
    cs = np.arange(ncmp_pad)[None, :] * CMP_STRIDE
    ss = np.arange(nslc)[:, None] * SLC_BLOCK
    ov = np.maximum(np.minimum(cs + CMP_BLOCK, ss + SLC_BLOCK) - np.maximum(cs, ss), 0) / CMP_BLOCK
    return jnp.asarray(ov, F32)


def _split_in_weights(w_in):
    bounds = np.cumsum(IN_SIZES)[:-1].tolist()
    (nq, ck, cv, sk, sv, wk, wv, ng, dq, dk, dv, hq, hf, hi, hg) = jnp.split(w_in, bounds, axis=-1)
    gcols = 3 * NSA_GROUP
    ng = jnp.concatenate(
        [jnp.pad(ng[:, g * gcols:(g + 1) * gcols], ((0, 0), (0, GATE_ROWS - gcols))) for g in range(2)]
        + [jnp.zeros((ng.shape[0], LANES - 2 * GATE_ROWS), ng.dtype)], axis=1)
    return jnp.concatenate([nq, ck, cv, sk, sv, wk, wv, dq, dk, dv, hq, hf, hi, hg, ng],
                           axis=1).astype(BF16)


def kernel(x, p, rel_bias, hgrn_lb_logits, ffn1_norm, ffn1_w_gate, ffn1_w_up, ffn1_w_down, mix_norm, w_in, w_out, nsa_q_norm, nsa_k_norm, nsa_cmp_pos, nsa_cmp_w1, nsa_cmp_w2, diff_q_norm, diff_k_norm, diff_lambda, diff_subln, hgrn_out_norm, ffn2_norm, ffn2_w_gate, ffn2_w_up, ffn2_w_down, ple_norm, ple_w_gate, ple_w_proj, ple_post_norm):
    batch, seq_len, _ = x.shape
    depth = p.shape[0]
    n = batch * seq_len
    assert seq_len % TOKEN_TILE == 0 and N_SELECT <= seq_len // SLC_BLOCK <= LANES
    nh = seq_len // CMP_STRIDE

    lb_all = jnp.cumsum(jax.nn.softmax(hgrn_lb_logits.astype(F32), axis=0), axis=0)
    lb_all = lb_all - lb_all[0:1]

    nsa_tiles = _bias_tiles(rel_bias[:, :NSA_HEADS], True)
    nsa_bias = nsa_tiles.reshape(2, NSA_GROUP, 4, AT, AT).transpose(0, 2, 3, 1, 4).reshape(
        2, 4, AT, NSA_GROUP * AT)
    diff_tiles = _bias_tiles(rel_bias[:, NSA_HEADS:], False)
    diff_bias = jnp.repeat(diff_tiles, 2, axis=0).reshape(2, 4, 3, AT, AT).transpose(
        0, 2, 3, 1, 4).reshape(2, 3, AT, 4 * AT)
    overlap_t = _overlap_matrix_t(nh, LANES)
    hgrn_masks = jnp.asarray(_hgrn_level_masks())

    xf = x.reshape(n, D_MODEL)
    pf = p.reshape(depth, n, PLE_DIM)
    row = lambda v: v.reshape(1, -1).astype(F32)
    for i in range(depth):
        xf = _ffn(xf, row(ffn1_norm[i]), ffn1_w_gate[i].astype(BF16), ffn1_w_up[i].astype(BF16),
                  ffn1_w_down[i].astype(BF16))

        qg = row(jnp.tile(nsa_q_norm[i], NSA_HEADS)) * (HEAD_DIM ** -0.5 * LOG2E)
        dqg = row(jnp.tile(diff_q_norm[i], 2 * DIFF_HEADS)) * (DIFF_QK_DIM ** -0.5 * LOG2E)
        (qt, ckv, ska, svt, wk, wvt, gates_t, dqt, dk, dvt, hq, hf, hi, hg) = _inproj(
            xf, seq_len, row(mix_norm[i]), _split_in_weights(w_in[i]), qg,
            row(jnp.tile(nsa_k_norm[i, 1], 2)), row(jnp.tile(nsa_k_norm[i, 2], 2)), dqg,
            row(jnp.tile(diff_k_norm[i], 2 * DIFF_HEADS)))

        w1 = nsa_cmp_w1[i]
        w1cat = jnp.concatenate([w1[:, :1024], w1[:, 1024:]], axis=2).astype(BF16)
        pos = jnp.broadcast_to(nsa_cmp_pos[i].reshape(2, 2, 1, 1024), (2, 2, 8, 1024))
        cmp_k, cmp_vt = _compress(ckv.reshape(4, batch, nh, 1024), pos, w1cat,
                                  nsa_cmp_w2[i].astype(BF16), row(nsa_k_norm[i, 0]))
        o_nsa = _nsa(qt, cmp_k, cmp_vt, ska, svt, wk, wvt, gates_t, nsa_bias, overlap_t,
                     batch, seq_len)

        lam_init = 0.8 - 0.6 * math.exp(-0.3 * i)
        o_d = _diff(dqt, dk, dvt, diff_bias, diff_lambda[i].astype(F32),
                    row(jnp.tile(diff_subln[i], 2)), lam_init, batch, seq_len)

        o_h = _hgrn(hq, hf, hi, hg, row(lb_all[i]), row(jnp.tile(hgrn_out_norm[i], HGRN_HEADS)),
                    hgrn_masks, batch, seq_len)

        wo = w_out[i].astype(BF16)
        xf = _outproj(xf, o_nsa, o_d, o_h, wo[0:384].reshape(2, 192, D_MODEL), wo[384:640],
                      wo[640:1024])
        xf = _ffn(xf, row(ffn2_norm[i]), ffn2_w_gate[i].astype(BF16), ffn2_w_up[i].astype(BF16),
                  ffn2_w_down[i].astype(BF16))
        xf = _ple(xf, pf[i], row(ple_norm[i]), ple_w_gate[i].astype(BF16),
                  ple_w_proj[i].astype(BF16), row(ple_post_norm[i]))
    return xf.reshape(batch, seq_len, D_MODEL)
```

```python
import functools
import math

import numpy as np
import jax
import jax.numpy as jnp
from jax import lax
from jax.experimental import pallas as pl
from jax.experimental.pallas import tpu as pltpu

F32 = jnp.float32
BF16 = jnp.bfloat16
HIGHEST = lax.Precision.HIGHEST

D_MODEL = 1024
D_FF = 2816
PLE_DIM = 256
HEAD_DIM = 64
NUM_BUCKETS = 32
MAX_DISTANCE = 128
NEG = -1e30
FORCE = 1e6
EPS = 1e-6
NSA_HEADS = 6
NSA_KV_HEADS = 2
NSA_GROUP = 3
CMP_BLOCK = 32
CMP_STRIDE = 16
CMP_HIDDEN = 256
SLC_BLOCK = 64
N_SELECT = 16
WINDOW = 512
DIFF_HEADS = 4
DIFF_QK_DIM = 32
HGRN_HEADS = 6
IN_SIZES = (384, 128, 128, 128, 128, 128, 128, 18, 256, 256, 256, 384, 384, 384, 384)

LANES = 128
VMEM_LIMIT = 56 * 1024 * 1024
TOKEN_TILE = 512
FF_CHUNK = 256
AT = 256
ONES_ROWS = 16
HCHUNK = 128
LOG2E = math.log2(math.e)
SHIFT_ONLY_LIMIT = 60.0
NORM_SLACK = 1.01


def _cparams(*sem):
    return pltpu.CompilerParams(dimension_semantics=sem, vmem_limit_bytes=VMEM_LIMIT)


def _const_spec(shape):
    nd = len(shape)
    return pl.BlockSpec(shape, lambda *_: (0,) * nd, pipeline_mode=pl.Buffered(1))


def _rms(x, gain):
    ms = jnp.mean(x * x, axis=-1, keepdims=True)
    return x * lax.rsqrt(ms + EPS) * gain


def _dot(a, b):
    return jnp.dot(a, b, preferred_element_type=F32)


def _dot_nt(a, b):
    return lax.dot_general(a, b, (((1,), (1,)), ((), ())), preferred_element_type=F32)


def _dot_tn(a, b):
    return lax.dot_general(a, b, (((0,), (0,)), ((), ())), preferred_element_type=F32)


def _split_bf16(x, terms):
    pieces = []
    for _ in range(terms - 1):
        hi = x.astype(BF16)
        pieces.append(hi)
        x = x - hi.astype(F32)
    pieces.append(x.astype(BF16))
    return pieces


def _dot_wide_rhs(a, x, terms):
    return sum(_dot(a, piece) for piece in _split_bf16(x, terms))


def _dot_wide_lhs(x, b, terms):
    return sum(_dot(piece, b) for piece in _split_bf16(x, terms))


def _ffn_body(x_ref, g_ref, wg_ref, wu_ref, wd_ref, o_ref):
    x = x_ref[...]
    h = _rms(x, g_ref[...]).astype(BF16)
    acc = jnp.zeros_like(x)
    for c in range(D_FF // FF_CHUNK):
        sl = slice(c * FF_CHUNK, (c + 1) * FF_CHUNK)
        a = _dot(h, wg_ref[:, sl])
        u = _dot(h, wu_ref[:, sl])
        act = (a * jax.nn.sigmoid(a) * u).astype(BF16)
        acc = acc + _dot(act, wd_ref[sl, :])
    o_ref[...] = x + 0.5 * acc


def _ffn(x, gain, wg, wu, wd):
    n = x.shape[0]
    tm = min(TOKEN_TILE, n)
    return pl.pallas_call(
        _ffn_body,
        grid=(n // tm,),
        in_specs=[pl.BlockSpec((tm, D_MODEL), lambda i: (i, 0)),
                  _const_spec((1, D_MODEL)),
                  _const_spec((D_MODEL, D_FF)),
                  _const_spec((D_MODEL, D_FF)),
                  _const_spec((D_FF, D_MODEL))],
        out_specs=pl.BlockSpec((tm, D_MODEL), lambda i: (i, 0)),
        out_shape=jax.ShapeDtypeStruct((n, D_MODEL), F32),
        compiler_params=_cparams("parallel"),
        name="ffn",
    )(x, gain, wg, wu, wd)


_C_NQ, _C_CK, _C_CV, _C_SK, _C_SV, _C_WK, _C_WV = 0, 384, 512, 640, 768, 896, 1024
_C_DQ, _C_DK, _C_DV = 1152, 1408, 1664
_C_HQ, _C_HF, _C_HI, _C_HG = 1920, 2304, 2688, 3072
_C_NG = 3456
IN_PERM_WIDTH = 3584
GATE_ROWS = 16


def _group_mean_sq(z, group):
    r = lax.broadcasted_iota(jnp.int32, (LANES, LANES), 0) // group
    c = lax.broadcasted_iota(jnp.int32, (LANES, LANES), 1) // group
    bd = jnp.where(r == c, 1.0 / group, 0.0).astype(BF16)
    zz = z * z
    parts = [_dot_wide_lhs(zz[:, j * LANES:(j + 1) * LANES], bd, 3)
             for j in range(z.shape[1] // LANES)]
    return parts[0] if len(parts) == 1 else jnp.concatenate(parts, axis=1)


def _group_norm(z, gain, group):
    return z * lax.rsqrt(_group_mean_sq(z, group) + EPS) * gain


def _store_values_t(vt_ref, v, n_groups, width):
    tm = v.shape[0]
    vt = v.T.astype(BF16)
    ones = jnp.ones((ONES_ROWS, AT), BF16)
    for g in range(n_groups):
        for c in range(tm // AT):
            vt_ref[g, c] = jnp.concatenate(
                [vt[g * width:(g + 1) * width, c * AT:(c + 1) * AT], ones], axis=0)


def _inproj_body(seq_len, x_ref, g_ref, w_ref, qg_ref, skg_ref, wkg_ref, dqg_ref, dkg_ref,
                 qt_ref, ckv_ref, ska_ref, svt_ref, wk_ref, wvt_ref, gt_ref,
                 dqt_ref, dk_ref, dvt_ref, hq_ref, hf_ref, hi_ref, hg_ref):
    tm = x_ref.shape[0]
    h = _rms(x_ref[...], g_ref[...]).astype(BF16)

    def proj(c0, width):
        return _dot(h, w_ref[:, c0:c0 + width])

    qt_ref[...] = _group_norm(proj(_C_NQ, 384), qg_ref[...], HEAD_DIM).T.astype(BF16)

    ckv = proj(_C_CK, 256)
    for s in range(4):
        ckv_ref[s] = ckv[:, s * 64:(s + 1) * 64]

    skn = _group_norm(proj(_C_SK, 128), skg_ref[...], HEAD_DIM).astype(BF16)
    row = lax.broadcasted_iota(jnp.int32, (tm, LANES), 0) + pl.program_id(0) * tm
    blk = (row % seq_len) // SLC_BLOCK
    col = lax.broadcasted_iota(jnp.int32, (tm, LANES), 1)
    onehot = jnp.where(blk == col, 1.0, 0.0).astype(BF16)
    ones = jnp.ones((tm, ONES_ROWS), BF16)
    for g in range(2):
        ska_ref[g] = jnp.concatenate([onehot, skn[:, g * 64:(g + 1) * 64], ones], axis=1)
    _store_values_t(svt_ref, proj(_C_SV, 128), 2, HEAD_DIM)
    wkn = _group_norm(proj(_C_WK, 128), wkg_ref[...], HEAD_DIM).astype(BF16)
    for g in range(2):
        wk_ref[g] = jnp.concatenate([wkn[:, g * 64:(g + 1) * 64], ones], axis=1)
    _store_values_t(wvt_ref, proj(_C_WV, 128), 2, HEAD_DIM)

    gates_t = jax.nn.sigmoid(proj(_C_NG, 128)).T
    for g in range(2):
        gt_ref[g] = gates_t[g * GATE_ROWS:(g + 1) * GATE_ROWS]

    dqt_ref[...] = _group_norm(proj(_C_DQ, 256), dqg_ref[...], DIFF_QK_DIM).T.astype(BF16)
    dkn = _group_norm(proj(_C_DK, 256), dkg_ref[...], DIFF_QK_DIM).astype(BF16)
    for hp in range(2):
        dk_ref[hp] = jnp.concatenate([dkn[:, hp * LANES:(hp + 1) * LANES], ones], axis=1)
    _store_values_t(dvt_ref, proj(_C_DV, 256), 2, LANES)

    hq_ref[...] = proj(_C_HQ, 384)
    hf_ref[...] = proj(_C_HF, 384)
    hi_ref[...] = proj(_C_HI, 384)
    hg_ref[...] = proj(_C_HG, 384)


def _inproj(x, seq_len, gain, w, qg, skg, wkg, dqg, dkg):
    n = x.shape[0]
    tm = min(TOKEN_TILE, seq_len)

    def rows(width, dtype):
        return (pl.BlockSpec((tm, width), lambda i: (i, 0)), jax.ShapeDtypeStruct((n, width), dtype))

    def heads(nh, width, dtype):
        return (pl.BlockSpec((nh, tm, width), lambda i: (0, i, 0)),
                jax.ShapeDtypeStruct((nh, n, width), dtype))

    def cols(height, dtype):
        return (pl.BlockSpec((height, tm), lambda i: (0, i)), jax.ShapeDtypeStruct((height, n), dtype))

    def values_t(height):
        h = height + ONES_ROWS
        return (pl.BlockSpec((2, tm // AT, h, AT), lambda i: (0, i, 0, 0)),
                jax.ShapeDtypeStruct((2, n // AT, h, AT), BF16))

    gates_t = (pl.BlockSpec((2, GATE_ROWS, tm), lambda i: (0, 0, i)),
               jax.ShapeDtypeStruct((2, GATE_ROWS, n), F32))
    outs = [cols(384, BF16), heads(4, 64, F32), heads(2, LANES + HEAD_DIM + ONES_ROWS, BF16),
            values_t(HEAD_DIM), heads(2, HEAD_DIM + ONES_ROWS, BF16), values_t(HEAD_DIM), gates_t,
            cols(256, BF16), heads(2, LANES + ONES_ROWS, BF16), values_t(LANES),
            rows(384, F32), rows(384, F32), rows(384, F32), rows(384, F32)]
    return pl.pallas_call(
        functools.partial(_inproj_body, seq_len),
        grid=(n // tm,),
        in_specs=[pl.BlockSpec((tm, D_MODEL), lambda i: (i, 0)),
                  _const_spec((1, D_MODEL)),
                  _const_spec((D_MODEL, IN_PERM_WIDTH)),
                  _const_spec((1, 384)), _const_spec((1, 128)), _const_spec((1, 128)),
                  _const_spec((1, 256)), _const_spec((1, 256))],
        out_specs=[o[0] for o in outs],
        out_shape=[o[1] for o in outs],
        compiler_params=_cparams("parallel"),
        name="in_proj",
    )(x, gain, w, qg, skg, wkg, dqg, dkg)


def _gelu_tanh(x):
    return 0.5 * x * (1.0 + jnp.tanh(math.sqrt(2.0 / math.pi) * (x + 0.044715 * (x * x * x))))


def _compress_body(h_ref, pos_ref, w1_ref, w2_ref, kg_ref, o_ref, ot_ref):
    nh = h_ref.shape[0]
    ab = _dot(h_ref[...].astype(BF16), w1_ref[...])
    pt = _dot(pos_ref[0].astype(BF16), w1_ref[...])
    pb = _dot(pos_ref[1].astype(BF16), w1_ref[...])
    bias = pt[0:1, 0:CMP_HIDDEN] + pb[0:1, CMP_HIDDEN:2 * CMP_HIDDEN]
    nxt = pltpu.roll(ab[:, CMP_HIDDEN:2 * CMP_HIDDEN], nh - 1, 0)
    hid = _gelu_tanh(ab[:, 0:CMP_HIDDEN] + nxt + bias).astype(BF16)
    out = _dot(hid, w2_ref[...])
    is_key = pl.program_id(0) < 2
    out = jnp.where(is_key, _rms(out, kg_ref[...]), out)
    o_ref[...] = out.astype(o_ref.dtype)
    out_t = jnp.concatenate([out, jnp.zeros_like(out)], axis=1).T
    ot_ref[...] = out_t[0:HEAD_DIM].astype(ot_ref.dtype)


def _compress(ckv_halves, pos, w1cat, w2, kgain):
    _, b, nh, _ = ckv_halves.shape
    return pl.pallas_call(
        _compress_body,
        grid=(4, b),
        in_specs=[pl.BlockSpec((None, None, nh, 1024), lambda s, i: (s, i, 0, 0)),
                  pl.BlockSpec((None, 2, 8, 1024), lambda s, i: (s // 2, 0, 0, 0)),
                  pl.BlockSpec((None, 1024, 2 * CMP_HIDDEN), lambda s, i: (s // 2, 0, 0)),
                  pl.BlockSpec((None, CMP_HIDDEN, HEAD_DIM), lambda s, i: (s // 2, 0, 0)),
                  pl.BlockSpec((1, HEAD_DIM), lambda s, i: (0, 0))],
        out_specs=[pl.BlockSpec((None, None, nh, HEAD_DIM), lambda s, i: (s, i, 0, 0)),
                   pl.BlockSpec((None, None, HEAD_DIM, nh), lambda s, i: (s, i, 0, 0))],
        out_shape=[jax.ShapeDtypeStruct((4, b, nh, HEAD_DIM), BF16),
                   jax.ShapeDtypeStruct((4, b, HEAD_DIM, nh), BF16)],
        compiler_params=_cparams("parallel", "parallel"),
        name="compress",
    )(ckv_halves, pos, w1cat, w2, kgain)


def _sweep(k_ref, prefix_t, q_t, k_norm_max, bias_max, vt_ref, bias_ref, kind_of, lo, hi, qi,
           m_ref, acc_ref, sa_ref, sb_ref):
    qf = q_t.astype(F32)
    bound = jnp.sqrt(jnp.sum(qf * qf, axis=0, keepdims=True)) * k_norm_max + bias_max
    srow = lax.broadcasted_iota(jnp.int32, (ONES_ROWS, q_t.shape[1]), 0)
    shift = jnp.where(srow == 0, -bound, 0.0).astype(BF16)
    parts = ([prefix_t] if prefix_t is not None else []) + [q_t, shift]
    rhs = jnp.concatenate(parts, axis=0)
    n = hi - lo

    def produce(s_ref, j):
        off = pl.multiple_of(j * AT, AT)
        s_ref[...] = _dot(k_ref[pl.ds(off, AT), :], rhs) + bias_ref[kind_of(qi - j)]

    def consume_shifted(s_ref, j):
        vt = vt_ref[j]
        for c in range(rhs.shape[1] // AT):
            cs = slice(c * AT, (c + 1) * AT)
            acc_ref[:, cs] = acc_ref[:, cs] + _dot(vt, jnp.exp2(s_ref[:, cs]).astype(BF16))

    def consume_running_max(s_ref, j):
        vt = vt_ref[j]
        for c in range(rhs.shape[1] // AT):
            cs = slice(c * AT, (c + 1) * AT)
            m_old = m_ref[:, cs]
            m_new = jnp.maximum(m_old, jnp.max(s_ref[:, cs], axis=0, keepdims=True))
            m_ref[:, cs] = m_new
            p = jnp.exp2(s_ref[:, cs] - m_new).astype(BF16)
            acc_ref[:, cs] = jnp.exp2(m_old - m_new) * acc_ref[:, cs] + _dot(vt, p)

    def run(consume):
        acc_ref[...] = jnp.zeros(acc_ref.shape, F32)
        produce(sa_ref, lo)

        def pair(t, c):
            j = lo + 2 * t
            produce(sb_ref, j + 1)
            consume(sa_ref, j)
            produce(sa_ref, jnp.minimum(j + 2, hi - 1))
            consume(sb_ref, j + 1)
            return c

        lax.fori_loop(0, n // 2, pair, 0)

        @pl.when(n % 2 == 1)
        def _():
            consume(sa_ref, hi - 1)

    shift_is_enough = jnp.max(bound) <= SHIFT_ONLY_LIMIT

    @pl.when(shift_is_enough)
    def _():
        run(consume_shifted)

    @pl.when(jnp.logical_not(shift_is_enough))
    def _():
        m_ref[...] = jnp.full(m_ref.shape, NEG, F32)
        run(consume_running_max)


def _nsa_body(q_ref, kc_ref, vct_ref, ska_ref, svt_ref, wk_ref, wvt_ref, gt_ref, bias_ref, ovl_ref,
              bnd_ref, o_ref, m_ref, acc_ref, sa_ref, sb_ref):
    qi = pl.program_id(2)
    q0 = qi * AT
    qblk = q_ref[...]
    q3t = jnp.concatenate([qblk[r * HEAD_DIM:(r + 1) * HEAD_DIM] for r in range(NSA_GROUP)],
                          axis=1)

    s_c = _dot(kc_ref[...], q3t)
    tpos = q0 + lax.broadcasted_iota(jnp.int32, s_c.shape, 1) % AT
    cend = lax.broadcasted_iota(jnp.int32, s_c.shape, 0) * CMP_STRIDE + (CMP_BLOCK - 1)
    s_c = jnp.where(cend <= tpos, s_c, NEG)
    p_c = jnp.exp2(s_c - jnp.max(s_c, axis=0, keepdims=True))
    p_c = p_c / jnp.sum(p_c, axis=0, keepdims=True)
    p_c = jnp.where(tpos >= CMP_BLOCK - 1, p_c, 0.0)
    o_ct = _dot(vct_ref[...], p_c.astype(BF16))

    p_sum = p_c[:, 0:AT] + p_c[:, AT:2 * AT] + p_c[:, 2 * AT:3 * AT]
    imp_t = _dot_wide_rhs(ovl_ref[...], p_sum, 3)
    nslc = imp_t.shape[0]
    blk = lax.broadcasted_iota(jnp.int32, imp_t.shape, 0)
    qpos = q0 + lax.broadcasted_iota(jnp.int32, imp_t.shape, 1)
    cur = qpos // SLC_BLOCK
    forced = (blk == 0) | (blk == cur) | (blk == cur - 1)
    imp_t = jnp.where(forced, FORCE, imp_t)
    imp_t = jnp.where(blk * SLC_BLOCK > qpos, NEG, imp_t)

    def pick_one(_, carry):
        work, sel = carry
        top = jnp.max(work, axis=0, keepdims=True)
        first = jnp.min(jnp.where(work == top, blk, nslc), axis=0, keepdims=True)
        pick = blk == first
        return jnp.where(pick, -jnp.inf, work), jnp.where(pick, 1.0, sel)

    _, sel_t = lax.fori_loop(0, N_SELECT, pick_one, (imp_t, jnp.zeros_like(imp_t)))
    sel_neg = jnp.where(sel_t > 0.5, 0.0, NEG).astype(BF16)

    bounds = bnd_ref[...]
    _sweep(ska_ref, jnp.concatenate([sel_neg] * NSA_GROUP, axis=1), q3t, bounds[:, 0:1],
           bounds[:, 2:3], svt_ref, bias_ref, lambda rel: jnp.minimum(rel, 2), 0, qi + 1, qi,
           m_ref, acc_ref, sa_ref, sb_ref)
    acc = acc_ref[...]
    o_st = acc[0:HEAD_DIM] / acc[HEAD_DIM:HEAD_DIM + 1]
    _sweep(wk_ref, None, q3t, bounds[:, 1:2], bounds[:, 2:3], wvt_ref, bias_ref,
           lambda rel: jnp.where(rel == WINDOW // AT, 3, rel),
           jnp.maximum(qi - WINDOW // AT, 0), qi + 1, qi, m_ref, acc_ref, sa_ref, sb_ref)
    acc = acc_ref[...]
    o_wt = acc[0:HEAD_DIM] / acc[HEAD_DIM:HEAD_DIM + 1]

    gt = gt_ref[...]
    rows = []
    for r in range(NSA_GROUP):
        cs = slice(r * AT, (r + 1) * AT)
        rows.append(gt[3 * r:3 * r + 1] * o_ct[:, cs] + gt[3 * r + 1:3 * r + 2] * o_st[:, cs]
                    + gt[3 * r + 2:3 * r + 3] * o_wt[:, cs])
    rows.append(jnp.zeros((AT - NSA_GROUP * HEAD_DIM, AT), F32))
    out = jnp.concatenate(rows, axis=0).T
    o_ref[...] = out[:, 0:NSA_GROUP * HEAD_DIM].astype(o_ref.dtype)


def _nsa(qt, cmp_k, cmp_vt, ska, svt, wk, wvt, gates_t, bias, overlap_t, bounds, batch, seq_len):
    n = batch * seq_len
    nq = seq_len // AT
    ncmp = cmp_k.shape[2]
    nslc = overlap_t.shape[0]
    cols = NSA_GROUP * AT
    vrows = HEAD_DIM + ONES_ROWS

    def seq_spec(width):
        return pl.BlockSpec((None, seq_len, width), lambda b, g, i: (g, b, 0))

    vt_spec = pl.BlockSpec((None, nq, vrows, AT), lambda b, g, i: (g, b, 0, 0))
    return pl.pallas_call(
        _nsa_body,
        grid=(batch, 2, nq),
        in_specs=[pl.BlockSpec((NSA_GROUP * HEAD_DIM, AT), lambda b, g, i: (g, b * nq + i)),
                  pl.BlockSpec((None, None, ncmp, HEAD_DIM), lambda b, g, i: (g, b, 0, 0)),
                  pl.BlockSpec((None, None, HEAD_DIM, ncmp), lambda b, g, i: (2 + g, b, 0, 0)),
                  seq_spec(LANES + HEAD_DIM + ONES_ROWS), vt_spec, seq_spec(HEAD_DIM + ONES_ROWS),
                  vt_spec,
                  pl.BlockSpec((None, GATE_ROWS, AT), lambda b, g, i: (g, 0, b * nq + i)),
                  pl.BlockSpec((None, 4, AT, cols), lambda b, g, i: (g, 0, 0, 0)),
                  pl.BlockSpec((nslc, ncmp), lambda b, g, i: (0, 0)),
                  pl.BlockSpec((1, LANES), lambda b, g, i: (0, 0))],
        out_specs=pl.BlockSpec((None, AT, NSA_GROUP * HEAD_DIM), lambda b, g, i: (g, b * nq + i, 0)),
        out_shape=jax.ShapeDtypeStruct((2, n, NSA_GROUP * HEAD_DIM), BF16),
        scratch_shapes=[pltpu.VMEM((1, cols), F32), pltpu.VMEM((vrows, cols), F32),
                        pltpu.VMEM((AT, cols), F32), pltpu.VMEM((AT, cols), F32)],
        compiler_params=_cparams("parallel", "parallel", "arbitrary"),
        name="nsa",
    )(qt, cmp_k, cmp_vt, ska, svt, wk, wvt, gates_t, bias, overlap_t, bounds)


def _diff_body(lam_init, q_ref, k_ref, vt_ref, bias_ref, lam_ref, sg_ref, bnd_ref, o_ref, m_ref,
               acc_ref, sa_ref, sb_ref):
    qi = pl.program_id(2)
    qblk = q_ref[...]
    row = lax.broadcasted_iota(jnp.int32, qblk.shape, 0)
    zero = jnp.zeros_like(qblk)
    rhs = jnp.concatenate([jnp.where(row // DIFF_QK_DIM == j, qblk, zero) for j in range(4)], axis=1)
    bounds = bnd_ref[...]
    _sweep(k_ref, None, rhs, bounds[:, 0:1], bounds[:, 1:2], vt_ref, bias_ref,
           lambda rel: jnp.minimum(rel, 2), 0, qi + 1, qi, m_ref, acc_ref, sa_ref, sb_ref)
    acc = acc_ref[...]
    o = acc[0:LANES] / acc[LANES:LANES + 1]
    lp = lam_ref[...]
    lam = (jnp.exp(jnp.sum(lp[0:1] * lp[1:2], axis=-1, keepdims=True))
           - jnp.exp(jnp.sum(lp[2:3] * lp[3:4], axis=-1, keepdims=True)) + lam_init)
    top = row < HEAD_DIM
    d_even = o[:, 0:AT] - lam * o[:, AT:2 * AT]
    d_odd = o[:, 2 * AT:3 * AT] - lam * o[:, 3 * AT:4 * AT]
    d = jnp.where(top, d_even, d_odd)
    d2 = d * d
    ms_top = jnp.sum(jnp.where(top, d2, 0.0), axis=0, keepdims=True)
    ms_bot = jnp.sum(jnp.where(top, 0.0, d2), axis=0, keepdims=True)
    ms = jnp.where(top, ms_top, ms_bot) * (1.0 / HEAD_DIM)
    out = (d * lax.rsqrt(ms + EPS)).T * sg_ref[...] * (1.0 - lam_init)
    o_ref[...] = out.astype(o_ref.dtype)


def _diff(dqt, dk, dvt, bias, lam_p, subln, bounds, lam_init, batch, seq_len):
    n = batch * seq_len
    nq = seq_len // AT
    cols = 4 * AT
    vrows = LANES + ONES_ROWS
    return pl.pallas_call(
        functools.partial(_diff_body, lam_init),
        grid=(batch, 2, nq),
        in_specs=[pl.BlockSpec((LANES, AT), lambda b, hp, i: (hp, b * nq + i)),
                  pl.BlockSpec((None, seq_len, LANES + ONES_ROWS), lambda b, hp, i: (hp, b, 0)),
                  pl.BlockSpec((None, nq, vrows, AT), lambda b, hp, i: (hp, b, 0, 0)),
                  pl.BlockSpec((None, 3, AT, cols), lambda b, hp, i: (hp, 0, 0, 0)),
                  pl.BlockSpec((4, DIFF_QK_DIM), lambda b, hp, i: (0, 0)),
                  pl.BlockSpec((1, LANES), lambda b, hp, i: (0, 0)),
                  pl.BlockSpec((1, LANES), lambda b, hp, i: (0, 0))],
        out_specs=pl.BlockSpec((AT, LANES), lambda b, hp, i: (b * nq + i, hp)),
        out_shape=jax.ShapeDtypeStruct((n, 2 * LANES), BF16),
        scratch_shapes=[pltpu.VMEM((1, cols), F32), pltpu.VMEM((vrows, cols), F32),
                        pltpu.VMEM((AT, cols), F32), pltpu.VMEM((AT, cols), F32)],
        compiler_params=_cparams("parallel", "parallel", "arbitrary"),
        name="diff",
    )(dqt, dk, dvt, bias, lam_p, subln, bounds)


def _hgrn_level_masks():
    t = np.arange(HCHUNK)[:, None]
    j = np.arange(HCHUNK)[None, :]
    mats = [(j <= t)]
    for lvl in range(1, 8):
        bs = (2 * HCHUNK) >> lvl
        mid = (t // bs) * bs + bs // 2
        mats.append(j < mid)
    return np.concatenate(mats, axis=0).astype(np.float32)


def _hgrn_body(hq_ref, hf_ref, hi_ref, hg_ref, lb_ref, gain_ref, mask_ref, o_ref, state_ref):
    @pl.when(pl.program_id(1) == 0)
    def _():
        state_ref[...] = jnp.zeros_like(state_ref)

    fr = hf_ref[...]
    lb = lb_ref[...]
    log_sig = jnp.minimum(fr, 0.0) - jnp.log(1.0 + jnp.exp(-jnp.abs(fr)))
    a = jnp.log(lb)
    b = jnp.log1p(-lb) + log_sig
    log_f = jnp.maximum(a, b) + jnp.log(1.0 + jnp.exp(-jnp.abs(a - b)))
    kk = (1.0 - lb) * jax.nn.sigmoid(-fr)
    hq = hq_ref[...]
    qq = hq * jax.nn.sigmoid(hq)
    vv = hi_ref[...].astype(BF16)

    sums = _dot_wide_rhs(mask_ref[...], log_f, 3)
    cum = sums[0:HCHUNK]
    last = cum[HCHUNK - 1:HCHUNK]
    q_in = (qq * jnp.exp(cum)).astype(BF16)
    k_out = (kk * jnp.exp(last - cum)).astype(BF16)
    decay = jnp.exp(last)

    row = lax.broadcasted_iota(jnp.int32, (HCHUNK, 3 * LANES), 0)
    trow = lax.broadcasted_iota(jnp.int32, (2 * HCHUNK, HCHUNK), 0) % HCHUNK
    scol = lax.broadcasted_iota(jnp.int32, (2 * HCHUNK, HCHUNK), 1)
    lane = lax.broadcasted_iota(jnp.int32, (HCHUNK, LANES), 1)
    lo_half = lane < HEAD_DIM
    sq_r = lax.broadcasted_iota(jnp.int32, (LANES, LANES), 0) // HEAD_DIM
    sq_c = lax.broadcasted_iota(jnp.int32, (LANES, LANES), 1) // HEAD_DIM
    same_head = sq_r == sq_c

    def split_heads(x):
        z = jnp.zeros_like(x)
        return jnp.concatenate([jnp.where(lo_half, x, z), jnp.where(lo_half, z, x)], axis=0)

    q_lv, k_lv = [qq.astype(BF16)], [kk.astype(BF16)]
    for lvl in range(1, 8):
        bs = (2 * HCHUNK) >> lvl
        ref = sums[lvl * HCHUNK:(lvl + 1) * HCHUNK]
        upper = (row % bs) >= bs // 2
        q_lv.append((qq * jnp.exp(jnp.where(upper, cum - ref, -jnp.inf))).astype(BF16))
        k_lv.append((kk * jnp.exp(jnp.where(upper, -jnp.inf, ref - cum))).astype(BF16))

    outs = []
    for pr in range(HGRN_HEADS // 2):
        ls = slice(pr * LANES, (pr + 1) * LANES)
        scores = jnp.where(trow == scol, _dot_nt(split_heads(q_lv[0][:, ls]), k_lv[0][:, ls]), 0.0)
        for lvl in range(1, 8):
            bs = (2 * HCHUNK) >> lvl
            s_l = _dot_nt(split_heads(q_lv[lvl][:, ls]), k_lv[lvl][:, ls])
            scores = scores + jnp.where(trow // bs == scol // bs, s_l, 0.0)
        intra2 = _dot(scores.astype(BF16), vv[:, ls])
        intra = jnp.where(lo_half, intra2[0:HCHUNK], intra2[HCHUNK:2 * HCHUNK])
        st = state_ref[pr]
        inter = _dot_nt(q_in[:, ls], st.astype(BF16))
        upd = _dot_tn(vv[:, ls], k_out[:, ls])
        state_ref[pr] = st * decay[:, ls] + jnp.where(same_head, upd, 0.0)
        o = inter + intra
        o2 = o * o
        ms_lo = jnp.sum(jnp.where(lo_half, o2, 0.0), axis=-1, keepdims=True)
        ms_hi = jnp.sum(jnp.where(lo_half, 0.0, o2), axis=-1, keepdims=True)
        ms = jnp.where(lo_half, ms_lo, ms_hi) * (1.0 / HEAD_DIM)
        outs.append(o * lax.rsqrt(ms + EPS))
    hg = hg_ref[...]
    out = jnp.concatenate(outs, axis=1) * gain_ref[...] * (hg * jax.nn.sigmoid(hg))
    o_ref[...] = out.astype(o_ref.dtype)


def _hgrn(hq, hf, hi, hg, lb, gain, masks, batch, seq_len):
    n = batch * seq_len
    nc = seq_len // HCHUNK
    w = HGRN_HEADS * HEAD_DIM
    row_spec = pl.BlockSpec((HCHUNK, w), lambda b, c: (b * nc + c, 0))
    return pl.pallas_call(
        _hgrn_body,
        grid=(batch, nc),
        in_specs=[row_spec, row_spec, row_spec, row_spec,
                  pl.BlockSpec((1, w), lambda b, c: (0, 0)),
                  pl.BlockSpec((1, w), lambda b, c: (0, 0)),
                  pl.BlockSpec((8 * HCHUNK, HCHUNK), lambda b, c: (0, 0))],
        out_specs=row_spec,
        out_shape=jax.ShapeDtypeStruct((n, w), BF16),
        scratch_shapes=[pltpu.VMEM((HGRN_HEADS // 2, LANES, LANES), F32)],
        compiler_params=_cparams("parallel", "arbitrary"),
        name="hgrn",
    )(hq, hf, hi, hg, lb, gain, masks)


def _outproj_body(x_ref, on_ref, od_ref, oh_ref, wn_ref, wd_ref, wh_ref, o_ref):
    y = _dot(on_ref[0], wn_ref[0]) + _dot(on_ref[1], wn_ref[1])
    y = y + _dot(od_ref[...], wd_ref[...]) + _dot(oh_ref[...], wh_ref[...])
    o_ref[...] = x_ref[...] + y


def _outproj(x, o_nsa, o_d, o_h, wn, wd, wh):
    n = x.shape[0]
    tm = min(TOKEN_TILE, n)
    return pl.pallas_call(
        _outproj_body,
        grid=(n // tm,),
        in_specs=[pl.BlockSpec((tm, D_MODEL), lambda i: (i, 0)),
                  pl.BlockSpec((2, tm, 192), lambda i: (0, i, 0)),
                  pl.BlockSpec((tm, 256), lambda i: (i, 0)),
                  pl.BlockSpec((tm, 384), lambda i: (i, 0)),
                  _const_spec((2, 192, D_MODEL)), _const_spec((256, D_MODEL)),
                  _const_spec((384, D_MODEL))],
        out_specs=pl.BlockSpec((tm, D_MODEL), lambda i: (i, 0)),
        out_shape=jax.ShapeDtypeStruct((n, D_MODEL), F32),
        compiler_params=_cparams("parallel"),
        name="out_proj",
    )(x, o_nsa, o_d, o_h, wn, wd, wh)


def _ple_body(x_ref, p_ref, g_ref, wg_ref, wp_ref, pg_ref, o_ref):
    x = x_ref[...]
    gate = jax.nn.sigmoid(_dot(_rms(x, g_ref[...]).astype(BF16), wg_ref[...]))
    emb = _rms(_dot(p_ref[...].astype(BF16), wp_ref[...]), pg_ref[...])
    o_ref[...] = x + gate * emb


def _ple(x, p, gain, wg, wp, pgain):
    n = x.shape[0]
    tm = min(TOKEN_TILE, n)
    return pl.pallas_call(
        _ple_body,
        grid=(n // tm,),
        in_specs=[pl.BlockSpec((tm, D_MODEL), lambda i: (i, 0)),
                  pl.BlockSpec((tm, PLE_DIM), lambda i: (i, 0)),
                  _const_spec((1, D_MODEL)), _const_spec((D_MODEL, D_MODEL)),
                  _const_spec((PLE_DIM, D_MODEL)), _const_spec((1, D_MODEL))],
        out_specs=pl.BlockSpec((tm, D_MODEL), lambda i: (i, 0)),
        out_shape=jax.ShapeDtypeStruct((n, D_MODEL), F32),
        compiler_params=_cparams("parallel"),
        name="ple",
    )(x, p, gain, wg, wp, pgain)


def _t5_bucket(dist):
    n = jnp.maximum(dist, 0)
    max_exact = NUM_BUCKETS // 2
    nf = jnp.maximum(n, 1).astype(F32)
    far = max_exact + (jnp.log(nf / max_exact) / math.log(MAX_DISTANCE / max_exact)
                       * (NUM_BUCKETS - max_exact)).astype(jnp.int32)
    return jnp.where(n < max_exact, n, jnp.minimum(far, NUM_BUCKETS - 1))


def _bias_tiles(tab, with_window):
    tab = (tab - tab[NUM_BUCKETS - 1:NUM_BUCKETS]).T * LOG2E
    d = jnp.arange(AT)[None, :] - jnp.arange(AT)[:, None]

    def lookup(dist):
        onehot = (_t5_bucket(dist)[..., None] == jnp.arange(NUM_BUCKETS)).astype(F32)
        return jnp.einsum("kqb,hb->hkq", onehot, tab, precision=HIGHEST)

    diag = jnp.where(d >= 0, lookup(d), NEG)
    near = lookup(d + AT)
    kinds = [diag, near, jnp.zeros_like(near)]
    if with_window:
        edge = jnp.where(d < 0, 0.0, NEG).astype(F32)
        kinds.append(jnp.broadcast_to(edge, near.shape))
    return jnp.stack(kinds, axis=1).astype(F32)


def _bounds_row(*vals):
    v = jnp.stack([jnp.asarray(x, F32) for x in vals])
    return jnp.pad(v, (0, LANES - v.shape[0])).reshape(1, LANES)


def _overlap_matrix_t(ncmp_pad, nslc):
    cs = np.arange(ncmp_pad)[None, :] * CMP_STRIDE
    ss = np.arange(nslc)[:, None] * SLC_BLOCK
    ov = np.maximum(np.minimum(cs + CMP_BLOCK, ss + SLC_BLOCK) - np.maximum(cs, ss), 0) / CMP_BLOCK
    return jnp.asarray(ov, BF16)


def _split_in_weights(w_in):
    bounds = np.cumsum(IN_SIZES)[:-1].tolist()
    (nq, ck, cv, sk, sv, wk, wv, ng, dq, dk, dv, hq, hf, hi, hg) = jnp.split(w_in, bounds, axis=-1)
    gcols = 3 * NSA_GROUP
    ng = jnp.concatenate(
        [jnp.pad(ng[:, g * gcols:(g + 1) * gcols], ((0, 0), (0, GATE_ROWS - gcols))) for g in range(2)]
        + [jnp.zeros((ng.shape[0], LANES - 2 * GATE_ROWS), ng.dtype)], axis=1)
    return jnp.concatenate([nq, ck, cv, sk, sv, wk, wv, dq, dk, dv, hq, hf, hi, hg, ng],
                           axis=1).astype(BF16)


def kernel(x, p, rel_bias, hgrn_lb_logits, ffn1_norm, ffn1_w_gate, ffn1_w_up, ffn1_w_down, mix_norm, w_in, w_out, nsa_q_norm, nsa_k_norm, nsa_cmp_pos, nsa_cmp_w1, nsa_cmp_w2, diff_q_norm, diff_k_norm, diff_lambda, diff_subln, hgrn_out_norm, ffn2_norm, ffn2_w_gate, ffn2_w_up, ffn2_w_down, ple_norm, ple_w_gate, ple_w_proj, ple_post_norm):
    batch, seq_len, _ = x.shape
    depth = p.shape[0]
    n = batch * seq_len
    assert seq_len % TOKEN_TILE == 0 and N_SELECT <= seq_len // SLC_BLOCK <= LANES
    nh = seq_len // CMP_STRIDE

    lb_all = jnp.cumsum(jax.nn.softmax(hgrn_lb_logits.astype(F32), axis=0), axis=0)
    lb_all = lb_all - lb_all[0:1]

    nsa_tiles = _bias_tiles(rel_bias[:, :NSA_HEADS], True)
    nsa_bias = nsa_tiles.reshape(2, NSA_GROUP, 4, AT, AT).transpose(0, 2, 3, 1, 4).reshape(
        2, 4, AT, NSA_GROUP * AT)
    diff_tiles = _bias_tiles(rel_bias[:, NSA_HEADS:], False)
    diff_bias = jnp.repeat(diff_tiles, 2, axis=0).reshape(2, 4, 3, AT, AT).transpose(
        0, 2, 3, 1, 4).reshape(2, 3, AT, 4 * AT)
    overlap_t = _overlap_matrix_t(nh, LANES)
    hgrn_masks = jnp.asarray(_hgrn_level_masks(), BF16)
    shifted_tab = (rel_bias - rel_bias[NUM_BUCKETS - 1:NUM_BUCKETS]).astype(F32) * LOG2E
    nsa_bias_max = jnp.maximum(jnp.max(shifted_tab[:, :NSA_HEADS]), 0.0)
    diff_bias_max = jnp.maximum(jnp.max(shifted_tab[:, NSA_HEADS:]), 0.0)

    xf = x.reshape(n, D_MODEL)
    pf = p.reshape(depth, n, PLE_DIM)
    row = lambda v: v.reshape(1, -1).astype(F32)
    for i in range(depth):
        xf = _ffn(xf, row(ffn1_norm[i]), ffn1_w_gate[i].astype(BF16), ffn1_w_up[i].astype(BF16),
                  ffn1_w_down[i].astype(BF16))

        qg = row(jnp.tile(nsa_q_norm[i], NSA_HEADS)) * (HEAD_DIM ** -0.5 * LOG2E)
        dqg = row(jnp.tile(diff_q_norm[i], 2 * DIFF_HEADS)) * (DIFF_QK_DIM ** -0.5 * LOG2E)
        (qt, ckv, ska, svt, wk, wvt, gates_t, dqt, dk, dvt, hq, hf, hi, hg) = _inproj(
            xf, seq_len, row(mix_norm[i]), _split_in_weights(w_in[i]), qg,
            row(jnp.tile(nsa_k_norm[i, 1], 2)), row(jnp.tile(nsa_k_norm[i, 2], 2)), dqg,
            row(jnp.tile(diff_k_norm[i], 2 * DIFF_HEADS)))

        w1 = nsa_cmp_w1[i]
        w1cat = jnp.concatenate([w1[:, :1024], w1[:, 1024:]], axis=2).astype(BF16)
        pos = jnp.broadcast_to(nsa_cmp_pos[i].reshape(2, 2, 1, 1024), (2, 2, 8, 1024))
        cmp_k, cmp_vt = _compress(ckv.reshape(4, batch, nh, 1024), pos, w1cat,
                                  nsa_cmp_w2[i].astype(BF16), row(nsa_k_norm[i, 0]))
        key_norm = lambda g, dim: NORM_SLACK * math.sqrt(dim) * jnp.max(jnp.abs(g))
        nsa_bounds = _bounds_row(key_norm(nsa_k_norm[i, 1], HEAD_DIM),
                                 key_norm(nsa_k_norm[i, 2], HEAD_DIM), nsa_bias_max)
        o_nsa = _nsa(qt, cmp_k, cmp_vt, ska, svt, wk, wvt, gates_t, nsa_bias, overlap_t,
                     nsa_bounds, batch, seq_len)

        lam_init = 0.8 - 0.6 * math.exp(-0.3 * i)
        diff_bounds = _bounds_row(key_norm(diff_k_norm[i], DIFF_QK_DIM), diff_bias_max)
        o_d = _diff(dqt, dk, dvt, diff_bias, diff_lambda[i].astype(F32),
                    row(jnp.tile(diff_subln[i], 2)), diff_bounds, lam_init, batch, seq_len)

        o_h = _hgrn(hq, hf, hi, hg, row(lb_all[i]), row(jnp.tile(hgrn_out_norm[i], HGRN_HEADS)),
                    hgrn_masks, batch, seq_len)

        wo = w_out[i].astype(BF16)
        xf = _outproj(xf, o_nsa, o_d, o_h, wo[0:384].reshape(2, 192, D_MODEL), wo[384:640],
                      wo[640:1024])
        xf = _ffn(xf, row(ffn2_norm[i]), ffn2_w_gate[i].astype(BF16), ffn2_w_up[i].astype(BF16),
                  ffn2_w_down[i].astype(BF16))
        xf = _ple(xf, pf[i], row(ple_norm[i]), ple_w_gate[i].astype(BF16),
                  ple_w_proj[i].astype(BF16), row(ple_post_norm[i]))
    return xf.reshape(batch, seq_len, D_MODEL)
```

```python
import functools
import math

import numpy as np
import jax
import jax.numpy as jnp
from jax import lax
from jax.experimental import pallas as pl
from jax.experimental.pallas import tpu as pltpu

F32 = jnp.float32
BF16 = jnp.bfloat16
HIGHEST = lax.Precision.HIGHEST

D_MODEL = 1024
D_FF = 2816
PLE_DIM = 256
HEAD_DIM = 64
NUM_BUCKETS = 32
MAX_DISTANCE = 128
NEG = -1e30
FORCE = 1e6
EPS = 1e-6
NSA_HEADS = 6
NSA_KV_HEADS = 2
NSA_GROUP = 3
CMP_BLOCK = 32
CMP_STRIDE = 16
CMP_HIDDEN = 256
SLC_BLOCK = 64
N_SELECT = 16
WINDOW = 512
DIFF_HEADS = 4
DIFF_QK_DIM = 32
HGRN_HEADS = 6
IN_SIZES = (384, 128, 128, 128, 128, 128, 128, 18, 256, 256, 256, 384, 384, 384, 384)

LANES = 128
VMEM_LIMIT = 56 * 1024 * 1024
TOKEN_TILE = 512
FF_CHUNK = 256
AT = 256
ONES_ROWS = 16
HCHUNK = 128
LOG2E = math.log2(math.e)
SHIFT_ONLY_LIMIT = 60.0
NORM_SLACK = 1.01


def _cparams(*sem):
    return pltpu.CompilerParams(dimension_semantics=sem, vmem_limit_bytes=VMEM_LIMIT)


def _const_spec(shape):
    nd = len(shape)
    return pl.BlockSpec(shape, lambda *_: (0,) * nd, pipeline_mode=pl.Buffered(1))


def _rms(x, gain):
    ms = jnp.mean(x * x, axis=-1, keepdims=True)
    return x * lax.rsqrt(ms + EPS) * gain


def _dot(a, b):
    return jnp.dot(a, b, preferred_element_type=F32)


def _dot_nt(a, b):
    return lax.dot_general(a, b, (((1,), (1,)), ((), ())), preferred_element_type=F32)


def _dot_tn(a, b):
    return lax.dot_general(a, b, (((0,), (0,)), ((), ())), preferred_element_type=F32)


def _split_bf16(x, terms):
    pieces = []
    for _ in range(terms - 1):
        hi = x.astype(BF16)
        pieces.append(hi)
        x = x - hi.astype(F32)
    pieces.append(x.astype(BF16))
    return pieces


def _dot_wide_rhs(a, x, terms):
    return sum(_dot(a, piece) for piece in _split_bf16(x, terms))


def _dot_wide_lhs(x, b, terms):
    return sum(_dot(piece, b) for piece in _split_bf16(x, terms))


def _ffn_body(x_ref, g_ref, wg_ref, wu_ref, wd_ref, o_ref):
    x = x_ref[...]
    h = _rms(x, g_ref[...]).astype(BF16)
    acc = jnp.zeros_like(x)
    for c in range(D_FF // FF_CHUNK):
        sl = slice(c * FF_CHUNK, (c + 1) * FF_CHUNK)
        a = _dot(h, wg_ref[:, sl])
        u = _dot(h, wu_ref[:, sl])
        act = (a * jax.nn.sigmoid(a) * u).astype(BF16)
        acc = acc + _dot(act, wd_ref[sl, :])
    o_ref[...] = x + 0.5 * acc


def _ffn(x, gain, wg, wu, wd):
    n = x.shape[0]
    tm = min(TOKEN_TILE, n)
    return pl.pallas_call(
        _ffn_body,
        grid=(n // tm,),
        in_specs=[pl.BlockSpec((tm, D_MODEL), lambda i: (i, 0)),
                  _const_spec((1, D_MODEL)),
                  _const_spec((D_MODEL, D_FF)),
                  _const_spec((D_MODEL, D_FF)),
                  _const_spec((D_FF, D_MODEL))],
        out_specs=pl.BlockSpec((tm, D_MODEL), lambda i: (i, 0)),
        out_shape=jax.ShapeDtypeStruct((n, D_MODEL), F32),
        compiler_params=_cparams("parallel"),
        name="ffn",
    )(x, gain, wg, wu, wd)


_C_NQ, _C_NG, _C_CK, _C_SK, _C_WK = 0, 384, 512, 768, 1024
_C_DQ, _C_DK, _C_DV = 1280, 1536, 1792
_C_HQ, _C_HI = 2048, 2816
IN_PERM_WIDTH = 3584
GATE_ROWS = 16


def _group_mean_sq(z, group):
    r = lax.broadcasted_iota(jnp.int32, (LANES, LANES), 0) // group
    c = lax.broadcasted_iota(jnp.int32, (LANES, LANES), 1) // group
    bd = jnp.where(r == c, 1.0 / group, 0.0).astype(BF16)
    zz = z * z
    parts = [_dot_wide_lhs(zz[:, j * LANES:(j + 1) * LANES], bd, 3)
             for j in range(z.shape[1] // LANES)]
    return parts[0] if len(parts) == 1 else jnp.concatenate(parts, axis=1)


def _group_norm(z, gain, group):
    return z * lax.rsqrt(_group_mean_sq(z, group) + EPS) * gain


def _store_values_t(vt_ref, v, n_groups, width):
    tm = v.shape[0]
    vt = v.T.astype(BF16)
    ones = jnp.ones((ONES_ROWS, AT), BF16)
    for g in range(n_groups):
        for c in range(tm // AT):
            vt_ref[g, c] = jnp.concatenate(
                [vt[g * width:(g + 1) * width, c * AT:(c + 1) * AT], ones], axis=0)


def _inproj_body(seq_len, x_ref, g_ref, w_ref, qg_ref, skg_ref, wkg_ref, dqg_ref, dkg_ref,
                 qt_ref, ckv_ref, ska_ref, svt_ref, wk_ref, wvt_ref, gt_ref,
                 dqt_ref, dk_ref, dvt_ref, hq_ref, hf_ref, hi_ref, hg_ref):
    tm = x_ref.shape[0]
    h = _rms(x_ref[...], g_ref[...]).astype(BF16)

    def proj(c0, width):
        return _dot(h, w_ref[:, c0:c0 + width])

    z = proj(_C_NQ, 512)
    qt_ref[...] = _group_norm(z[:, 0:384], qg_ref[...], HEAD_DIM).T.astype(BF16)
    gates_t = jax.nn.sigmoid(z[:, 384:512]).T
    for g in range(2):
        gt_ref[g] = gates_t[g * GATE_ROWS:(g + 1) * GATE_ROWS]

    ckv = proj(_C_CK, 256)
    for s in range(4):
        ckv_ref[s] = ckv[:, s * 64:(s + 1) * 64]

    z_s = proj(_C_SK, 256)
    skn = _group_norm(z_s[:, 0:LANES], skg_ref[...], HEAD_DIM).astype(BF16)
    row = lax.broadcasted_iota(jnp.int32, (tm, LANES), 0) + pl.program_id(0) * tm
    blk = (row % seq_len) // SLC_BLOCK
    col = lax.broadcasted_iota(jnp.int32, (tm, LANES), 1)
    onehot = jnp.where(blk == col, 1.0, 0.0).astype(BF16)
    ones = jnp.ones((tm, ONES_ROWS), BF16)
    for g in range(2):
        ska_ref[g] = jnp.concatenate([onehot, skn[:, g * 64:(g + 1) * 64], ones], axis=1)
    _store_values_t(svt_ref, z_s[:, LANES:2 * LANES], 2, HEAD_DIM)
    z_w = proj(_C_WK, 256)
    wkn = _group_norm(z_w[:, 0:LANES], wkg_ref[...], HEAD_DIM).astype(BF16)
    for g in range(2):
        wk_ref[g] = jnp.concatenate([wkn[:, g * 64:(g + 1) * 64], ones], axis=1)
    _store_values_t(wvt_ref, z_w[:, LANES:2 * LANES], 2, HEAD_DIM)

    dqt_ref[...] = _group_norm(proj(_C_DQ, 256), dqg_ref[...], DIFF_QK_DIM).T.astype(BF16)
    dkn = _group_norm(proj(_C_DK, 256), dkg_ref[...], DIFF_QK_DIM).astype(BF16)
    for hp in range(2):
        dk_ref[hp] = jnp.concatenate([dkn[:, hp * LANES:(hp + 1) * LANES], ones], axis=1)
    _store_values_t(dvt_ref, proj(_C_DV, 256), 2, LANES)

    z_h = proj(_C_HQ, 768)
    hq_ref[...] = z_h[:, 0:384]
    hf_ref[...] = z_h[:, 384:768]
    z_h = proj(_C_HI, 768)
    hi_ref[...] = z_h[:, 0:384]
    hg_ref[...] = z_h[:, 384:768]


def _inproj(x, seq_len, gain, w, qg, skg, wkg, dqg, dkg):
    n = x.shape[0]
    tm = min(TOKEN_TILE, seq_len)

    def rows(width, dtype):
        return (pl.BlockSpec((tm, width), lambda i: (i, 0)), jax.ShapeDtypeStruct((n, width), dtype))

    def heads(nh, width, dtype):
        return (pl.BlockSpec((nh, tm, width), lambda i: (0, i, 0)),
                jax.ShapeDtypeStruct((nh, n, width), dtype))

    def cols(height, dtype):
        return (pl.BlockSpec((height, tm), lambda i: (0, i)), jax.ShapeDtypeStruct((height, n), dtype))

    def values_t(height):
        h = height + ONES_ROWS
        return (pl.BlockSpec((2, tm // AT, h, AT), lambda i: (0, i, 0, 0)),
                jax.ShapeDtypeStruct((2, n // AT, h, AT), BF16))

    gates_t = (pl.BlockSpec((2, GATE_ROWS, tm), lambda i: (0, 0, i)),
               jax.ShapeDtypeStruct((2, GATE_ROWS, n), F32))
    outs = [cols(384, BF16), heads(4, 64, F32), heads(2, LANES + HEAD_DIM + ONES_ROWS, BF16),
            values_t(HEAD_DIM), heads(2, HEAD_DIM + ONES_ROWS, BF16), values_t(HEAD_DIM), gates_t,
            cols(256, BF16), heads(2, LANES + ONES_ROWS, BF16), values_t(LANES),
            rows(384, F32), rows(384, F32), rows(384, F32), rows(384, F32)]
    return pl.pallas_call(
        functools.partial(_inproj_body, seq_len),
        grid=(n // tm,),
        in_specs=[pl.BlockSpec((tm, D_MODEL), lambda i: (i, 0)),
                  _const_spec((1, D_MODEL)),
                  _const_spec((D_MODEL, IN_PERM_WIDTH)),
                  _const_spec((1, 384)), _const_spec((1, 128)), _const_spec((1, 128)),
                  _const_spec((1, 256)), _const_spec((1, 256))],
        out_specs=[o[0] for o in outs],
        out_shape=[o[1] for o in outs],
        compiler_params=_cparams("parallel"),
        name="in_proj",
    )(x, gain, w, qg, skg, wkg, dqg, dkg)


def _gelu_tanh(x):
    return 0.5 * x * (1.0 + jnp.tanh(math.sqrt(2.0 / math.pi) * (x + 0.044715 * (x * x * x))))


def _compress_body(h_ref, pos_ref, w1_ref, w2_ref, kg_ref, o_ref, ot_ref):
    nh = h_ref.shape[0]
    ab = _dot(h_ref[...].astype(BF16), w1_ref[...])
    pt = _dot(pos_ref[0].astype(BF16), w1_ref[...])
    pb = _dot(pos_ref[1].astype(BF16), w1_ref[...])
    bias = pt[0:1, 0:CMP_HIDDEN] + pb[0:1, CMP_HIDDEN:2 * CMP_HIDDEN]
    nxt = pltpu.roll(ab[:, CMP_HIDDEN:2 * CMP_HIDDEN], nh - 1, 0)
    hid = _gelu_tanh(ab[:, 0:CMP_HIDDEN] + nxt + bias).astype(BF16)
    out = _dot(hid, w2_ref[...])
    is_key = pl.program_id(0) < 2
    out = jnp.where(is_key, _rms(out, kg_ref[...]), out)
    o_ref[...] = out.astype(o_ref.dtype)
    out_t = jnp.concatenate([out, jnp.zeros_like(out)], axis=1).T
    ot_ref[...] = out_t[0:HEAD_DIM].astype(ot_ref.dtype)


def _compress(ckv_halves, pos, w1cat, w2, kgain):
    _, b, nh, _ = ckv_halves.shape
    return pl.pallas_call(
        _compress_body,
        grid=(4, b),
        in_specs=[pl.BlockSpec((None, None, nh, 1024), lambda s, i: (s, i, 0, 0)),
                  pl.BlockSpec((None, 2, 8, 1024), lambda s, i: (s // 2, 0, 0, 0)),
                  pl.BlockSpec((None, 1024, 2 * CMP_HIDDEN), lambda s, i: (s // 2, 0, 0)),
                  pl.BlockSpec((None, CMP_HIDDEN, HEAD_DIM), lambda s, i: (s // 2, 0, 0)),
                  pl.BlockSpec((1, HEAD_DIM), lambda s, i: (0, 0))],
        out_specs=[pl.BlockSpec((None, None, nh, HEAD_DIM), lambda s, i: (s, i, 0, 0)),
                   pl.BlockSpec((None, None, HEAD_DIM, nh), lambda s, i: (s, i, 0, 0))],
        out_shape=[jax.ShapeDtypeStruct((4, b, nh, HEAD_DIM), BF16),
                   jax.ShapeDtypeStruct((4, b, HEAD_DIM, nh), BF16)],
        compiler_params=_cparams("parallel", "parallel"),
        name="compress",
    )(ckv_halves, pos, w1cat, w2, kgain)


def _sweep(k_ref, prefix_t, q_t, k_norm_max, bias_max, vt_ref, bias_ref, kind_of, lo, hi, qi,
           m_ref, acc_ref, sa_ref, sb_ref):
    qf = q_t.astype(F32)
    bound = jnp.sqrt(jnp.sum(qf * qf, axis=0, keepdims=True)) * k_norm_max + bias_max
    srow = lax.broadcasted_iota(jnp.int32, (ONES_ROWS, q_t.shape[1]), 0)
    shift = jnp.where(srow == 0, -bound, 0.0).astype(BF16)
    parts = ([prefix_t] if prefix_t is not None else []) + [q_t, shift]
    rhs = jnp.concatenate(parts, axis=0)
    n = hi - lo

    def produce(s_ref, j):
        off = pl.multiple_of(j * AT, AT)
        s_ref[...] = _dot(k_ref[pl.ds(off, AT), :], rhs) + bias_ref[kind_of(qi - j)]

    def consume_shifted(s_ref, j):
        vt = vt_ref[j]
        for c in range(rhs.shape[1] // AT):
            cs = slice(c * AT, (c + 1) * AT)
            acc_ref[:, cs] = acc_ref[:, cs] + _dot(vt, jnp.exp2(s_ref[:, cs]).astype(BF16))

    def consume_running_max(s_ref, j):
        vt = vt_ref[j]
        for c in range(rhs.shape[1] // AT):
            cs = slice(c * AT, (c + 1) * AT)
            m_old = m_ref[:, cs]
            m_new = jnp.maximum(m_old, jnp.max(s_ref[:, cs], axis=0, keepdims=True))
            m_ref[:, cs] = m_new
            p = jnp.exp2(s_ref[:, cs] - m_new).astype(BF16)
            acc_ref[:, cs] = jnp.exp2(m_old - m_new) * acc_ref[:, cs] + _dot(vt, p)

    def run(consume, pairs_per_trip):
        acc_ref[...] = jnp.zeros(acc_ref.shape, F32)
        produce(sa_ref, lo)

        def pairs(count):
            def body(t, j):
                for _ in range(count):
                    produce(sb_ref, j + 1)
                    consume(sa_ref, j)
                    produce(sa_ref, jnp.minimum(j + 2, hi - 1))
                    consume(sb_ref, j + 1)
                    j = j + 2
                return j
            return body

        tiles = 2 * pairs_per_trip
        j = lax.fori_loop(0, n // tiles, pairs(pairs_per_trip), lo)
        if pairs_per_trip > 1:
            j = lax.fori_loop(0, (n % tiles) // 2, pairs(1), j)

        @pl.when(n % 2 == 1)
        def _():
            consume(sa_ref, hi - 1)

    shift_is_enough = jnp.max(bound) <= SHIFT_ONLY_LIMIT

    @pl.when(shift_is_enough)
    def _():
        run(consume_shifted, 2)

    @pl.when(jnp.logical_not(shift_is_enough))
    def _():
        m_ref[...] = jnp.full(m_ref.shape, NEG, F32)
        run(consume_running_max, 1)


def _nsa_body(q_ref, kc_ref, vct_ref, ska_ref, svt_ref, wk_ref, wvt_ref, gt_ref, bias_ref, ovl_ref,
              bnd_ref, o_ref, m_ref, acc_ref, sa_ref, sb_ref):
    qi = pl.program_id(2)
    q0 = qi * AT
    qblk = q_ref[...]
    q3t = jnp.concatenate([qblk[r * HEAD_DIM:(r + 1) * HEAD_DIM] for r in range(NSA_GROUP)],
                          axis=1)

    def compressed(rows):
        s_c = _dot(kc_ref[0:rows, :], q3t)
        tpos = q0 + lax.broadcasted_iota(jnp.int32, s_c.shape, 1) % AT
        cend = lax.broadcasted_iota(jnp.int32, s_c.shape, 0) * CMP_STRIDE + (CMP_BLOCK - 1)
        s_c = jnp.where(cend <= tpos, s_c, NEG)
        p_c = jnp.exp2(s_c - jnp.max(s_c, axis=0, keepdims=True))
        p_c = p_c / jnp.sum(p_c, axis=0, keepdims=True)
        p_c = jnp.where(tpos >= CMP_BLOCK - 1, p_c, 0.0)
        p_sum = p_c[:, 0:AT] + p_c[:, AT:2 * AT] + p_c[:, 2 * AT:3 * AT]
        return (_dot(vct_ref[:, 0:rows], p_c.astype(BF16)),
                _dot_wide_rhs(ovl_ref[:, 0:rows], p_sum, 3))

    ncmp = kc_ref.shape[0]
    o_ct, imp_t = lax.cond(qi < pl.num_programs(2) // 2, lambda: compressed(ncmp // 2),
                           lambda: compressed(ncmp))

    nslc = imp_t.shape[0]
    blk = lax.broadcasted_iota(jnp.int32, imp_t.shape, 0)
    qpos = q0 + lax.broadcasted_iota(jnp.int32, imp_t.shape, 1)
    cur = qpos // SLC_BLOCK
    forced = (blk == 0) | (blk == cur) | (blk == cur - 1)
    imp_t = jnp.where(forced, FORCE, imp_t)
    imp_t = jnp.where(blk * SLC_BLOCK > qpos, NEG, imp_t)

    def pick_one(_, carry):
        work, sel = carry
        top = jnp.max(work, axis=0, keepdims=True)
        first = jnp.min(jnp.where(work == top, blk, nslc), axis=0, keepdims=True)
        pick = blk == first
        return jnp.where(pick, -jnp.inf, work), jnp.where(pick, 1.0, sel)

    _, sel_t = lax.fori_loop(0, N_SELECT, pick_one, (imp_t, jnp.zeros_like(imp_t)))
    sel_neg = jnp.where(sel_t > 0.5, 0.0, NEG).astype(BF16)

    bounds = bnd_ref[...]
    _sweep(ska_ref, jnp.concatenate([sel_neg] * NSA_GROUP, axis=1), q3t, bounds[:, 0:1],
           bounds[:, 2:3], svt_ref, bias_ref, lambda rel: jnp.minimum(rel, 2), 0, qi + 1, qi,
           m_ref, acc_ref, sa_ref, sb_ref)
    acc = acc_ref[...]
    o_st = acc[0:HEAD_DIM] / acc[HEAD_DIM:HEAD_DIM + 1]
    _sweep(wk_ref, None, q3t, bounds[:, 1:2], bounds[:, 2:3], wvt_ref, bias_ref,
           lambda rel: jnp.where(rel == WINDOW // AT, 3, rel),
           jnp.maximum(qi - WINDOW // AT, 0), qi + 1, qi, m_ref, acc_ref, sa_ref, sb_ref)
    acc = acc_ref[...]
    o_wt = acc[0:HEAD_DIM] / acc[HEAD_DIM:HEAD_DIM + 1]

    gt = gt_ref[...]
    rows = []
    for r in range(NSA_GROUP):
        cs = slice(r * AT, (r + 1) * AT)
        rows.append(gt[3 * r:3 * r + 1] * o_ct[:, cs] + gt[3 * r + 1:3 * r + 2] * o_st[:, cs]
                    + gt[3 * r + 2:3 * r + 3] * o_wt[:, cs])
    rows.append(jnp.zeros((AT - NSA_GROUP * HEAD_DIM, AT), F32))
    out = jnp.concatenate(rows, axis=0).T
    o_ref[...] = out[:, 0:NSA_GROUP * HEAD_DIM].astype(o_ref.dtype)


def _nsa(qt, cmp_k, cmp_vt, ska, svt, wk, wvt, gates_t, bias, overlap_t, bounds, batch, seq_len):
    n = batch * seq_len
    nq = seq_len // AT
    ncmp = cmp_k.shape[2]
    nslc = overlap_t.shape[0]
    cols = NSA_GROUP * AT
    vrows = HEAD_DIM + ONES_ROWS

    def seq_spec(width):
        return pl.BlockSpec((None, seq_len, width), lambda b, g, i: (g, b, 0))

    vt_spec = pl.BlockSpec((None, nq, vrows, AT), lambda b, g, i: (g, b, 0, 0))
    return pl.pallas_call(
        _nsa_body,
        grid=(batch, 2, nq),
        in_specs=[pl.BlockSpec((NSA_GROUP * HEAD_DIM, AT), lambda b, g, i: (g, b * nq + i)),
                  pl.BlockSpec((None, None, ncmp, HEAD_DIM), lambda b, g, i: (g, b, 0, 0)),
                  pl.BlockSpec((None, None, HEAD_DIM, ncmp), lambda b, g, i: (2 + g, b, 0, 0)),
                  seq_spec(LANES + HEAD_DIM + ONES_ROWS), vt_spec, seq_spec(HEAD_DIM + ONES_ROWS),
                  vt_spec,
                  pl.BlockSpec((None, GATE_ROWS, AT), lambda b, g, i: (g, 0, b * nq + i)),
                  pl.BlockSpec((None, 4, AT, cols), lambda b, g, i: (g, 0, 0, 0)),
                  pl.BlockSpec((nslc, ncmp), lambda b, g, i: (0, 0)),
                  pl.BlockSpec((1, LANES), lambda b, g, i: (0, 0))],
        out_specs=pl.BlockSpec((None, AT, NSA_GROUP * HEAD_DIM), lambda b, g, i: (g, b * nq + i, 0)),
        out_shape=jax.ShapeDtypeStruct((2, n, NSA_GROUP * HEAD_DIM), BF16),
        scratch_shapes=[pltpu.VMEM((1, cols), F32), pltpu.VMEM((vrows, cols), F32),
                        pltpu.VMEM((AT, cols), F32), pltpu.VMEM((AT, cols), F32)],
        compiler_params=_cparams("parallel", "parallel", "arbitrary"),
        name="nsa",
    )(qt, cmp_k, cmp_vt, ska, svt, wk, wvt, gates_t, bias, overlap_t, bounds)


def _diff_body(lam_init, q_ref, k_ref, vt_ref, bias_ref, lam_ref, sg_ref, bnd_ref, o_ref, m_ref,
               acc_ref, sa_ref, sb_ref):
    qi = pl.program_id(2)
    qblk = q_ref[...]
    row = lax.broadcasted_iota(jnp.int32, qblk.shape, 0)
    zero = jnp.zeros_like(qblk)
    rhs = jnp.concatenate([jnp.where(row // DIFF_QK_DIM == j, qblk, zero) for j in range(4)], axis=1)
    bounds = bnd_ref[...]
    _sweep(k_ref, None, rhs, bounds[:, 0:1], bounds[:, 1:2], vt_ref, bias_ref,
           lambda rel: jnp.minimum(rel, 2), 0, qi + 1, qi, m_ref, acc_ref, sa_ref, sb_ref)
    acc = acc_ref[...]
    o = acc[0:LANES] / acc[LANES:LANES + 1]
    lp = lam_ref[...]
    lam = (jnp.exp(jnp.sum(lp[0:1] * lp[1:2], axis=-1, keepdims=True))
           - jnp.exp(jnp.sum(lp[2:3] * lp[3:4], axis=-1, keepdims=True)) + lam_init)
    top = row < HEAD_DIM
    d_even = o[:, 0:AT] - lam * o[:, AT:2 * AT]
    d_odd = o[:, 2 * AT:3 * AT] - lam * o[:, 3 * AT:4 * AT]
    d = jnp.where(top, d_even, d_odd)
    d2 = d * d
    ms_top = jnp.sum(jnp.where(top, d2, 0.0), axis=0, keepdims=True)
    ms_bot = jnp.sum(jnp.where(top, 0.0, d2), axis=0, keepdims=True)
    ms = jnp.where(top, ms_top, ms_bot) * (1.0 / HEAD_DIM)
    out = (d * lax.rsqrt(ms + EPS)).T * sg_ref[...] * (1.0 - lam_init)
    o_ref[...] = out.astype(o_ref.dtype)


def _diff(dqt, dk, dvt, bias, lam_p, subln, bounds, lam_init, batch, seq_len):
    n = batch * seq_len
    nq = seq_len // AT
    cols = 4 * AT
    vrows = LANES + ONES_ROWS
    return pl.pallas_call(
        functools.partial(_diff_body, lam_init),
        grid=(batch, 2, nq),
        in_specs=[pl.BlockSpec((LANES, AT), lambda b, hp, i: (hp, b * nq + i)),
                  pl.BlockSpec((None, seq_len, LANES + ONES_ROWS), lambda b, hp, i: (hp, b, 0)),
                  pl.BlockSpec((None, nq, vrows, AT), lambda b, hp, i: (hp, b, 0, 0)),
                  pl.BlockSpec((None, 3, AT, cols), lambda b, hp, i: (hp, 0, 0, 0)),
                  pl.BlockSpec((4, DIFF_QK_DIM), lambda b, hp, i: (0, 0)),
                  pl.BlockSpec((1, LANES), lambda b, hp, i: (0, 0)),
                  pl.BlockSpec((1, LANES), lambda b, hp, i: (0, 0))],
        out_specs=pl.BlockSpec((AT, LANES), lambda b, hp, i: (b * nq + i, hp)),
        out_shape=jax.ShapeDtypeStruct((n, 2 * LANES), BF16),
        scratch_shapes=[pltpu.VMEM((1, cols), F32), pltpu.VMEM((vrows, cols), F32),
                        pltpu.VMEM((AT, cols), F32), pltpu.VMEM((AT, cols), F32)],
        compiler_params=_cparams("parallel", "parallel", "arbitrary"),
        name="diff",
    )(dqt, dk, dvt, bias, lam_p, subln, bounds)


def _hgrn_level_masks():
    t = np.arange(HCHUNK)[:, None]
    j = np.arange(HCHUNK)[None, :]
    mats = [(j <= t)]
    for lvl in range(1, 8):
        bs = (2 * HCHUNK) >> lvl
        mid = (t // bs) * bs + bs // 2
        mats.append(j < mid)
    return np.concatenate(mats, axis=0).astype(np.float32)


def _hgrn_body(hq_ref, hf_ref, hi_ref, hg_ref, lb_ref, gain_ref, mask_ref, o_ref, state_ref):
    @pl.when(pl.program_id(1) == 0)
    def _():
        state_ref[...] = jnp.zeros_like(state_ref)

    fr = hf_ref[...]
    lb = lb_ref[...]
    log_sig = jnp.minimum(fr, 0.0) - jnp.log(1.0 + jnp.exp(-jnp.abs(fr)))
    a = jnp.log(lb)
    b = jnp.log1p(-lb) + log_sig
    log_f = jnp.maximum(a, b) + jnp.log(1.0 + jnp.exp(-jnp.abs(a - b)))
    kk = (1.0 - lb) * jax.nn.sigmoid(-fr)
    hq = hq_ref[...]
    qq = hq * jax.nn.sigmoid(hq)
    vv = hi_ref[...].astype(BF16)

    sums = _dot_wide_rhs(mask_ref[...], log_f, 3)
    cum = sums[0:HCHUNK]
    last = cum[HCHUNK - 1:HCHUNK]
    q_in = (qq * jnp.exp(cum)).astype(BF16)
    k_out = (kk * jnp.exp(last - cum)).astype(BF16)
    decay = jnp.exp(last)

    row = lax.broadcasted_iota(jnp.int32, (HCHUNK, 3 * LANES), 0)
    trow = lax.broadcasted_iota(jnp.int32, (2 * HCHUNK, HCHUNK), 0) % HCHUNK
    scol = lax.broadcasted_iota(jnp.int32, (2 * HCHUNK, HCHUNK), 1)
    lane = lax.broadcasted_iota(jnp.int32, (HCHUNK, LANES), 1)
    lo_half = lane < HEAD_DIM
    sq_r = lax.broadcasted_iota(jnp.int32, (LANES, LANES), 0) // HEAD_DIM
    sq_c = lax.broadcasted_iota(jnp.int32, (LANES, LANES), 1) // HEAD_DIM
    same_head = sq_r == sq_c

    def split_heads(x):
        z = jnp.zeros_like(x)
        return jnp.concatenate([jnp.where(lo_half, x, z), jnp.where(lo_half, z, x)], axis=0)

    q_lv, k_lv = [qq.astype(BF16)], [kk.astype(BF16)]
    for lvl in range(1, 8):
        bs = (2 * HCHUNK) >> lvl
        ref = sums[lvl * HCHUNK:(lvl + 1) * HCHUNK]
        upper = (row % bs) >= bs // 2
        q_lv.append((qq * jnp.exp(jnp.where(upper, cum - ref, -jnp.inf))).astype(BF16))
        k_lv.append((kk * jnp.exp(jnp.where(upper, -jnp.inf, ref - cum))).astype(BF16))

    outs = []
    for pr in range(HGRN_HEADS // 2):
        ls = slice(pr * LANES, (pr + 1) * LANES)
        scores = jnp.where(trow == scol, _dot_nt(split_heads(q_lv[0][:, ls]), k_lv[0][:, ls]), 0.0)
        for lvl in range(1, 8):
            bs = (2 * HCHUNK) >> lvl
            s_l = _dot_nt(split_heads(q_lv[lvl][:, ls]), k_lv[lvl][:, ls])
            scores = scores + jnp.where(trow // bs == scol // bs, s_l, 0.0)
        intra2 = _dot(scores.astype(BF16), vv[:, ls])
        intra = jnp.where(lo_half, intra2[0:HCHUNK], intra2[HCHUNK:2 * HCHUNK])
        st = state_ref[pr]
        inter = _dot_nt(q_in[:, ls], st.astype(BF16))
        upd = _dot_tn(vv[:, ls], k_out[:, ls])
        state_ref[pr] = st * decay[:, ls] + jnp.where(same_head, upd, 0.0)
        o = inter + intra
        o2 = o * o
        ms_lo = jnp.sum(jnp.where(lo_half, o2, 0.0), axis=-1, keepdims=True)
        ms_hi = jnp.sum(jnp.where(lo_half, 0.0, o2), axis=-1, keepdims=True)
        ms = jnp.where(lo_half, ms_lo, ms_hi) * (1.0 / HEAD_DIM)
        outs.append(o * lax.rsqrt(ms + EPS))
    hg = hg_ref[...]
    out = jnp.concatenate(outs, axis=1) * gain_ref[...] * (hg * jax.nn.sigmoid(hg))
    o_ref[...] = out.astype(o_ref.dtype)


def _hgrn(hq, hf, hi, hg, lb, gain, masks, batch, seq_len):
    n = batch * seq_len
    nc = seq_len // HCHUNK
    w = HGRN_HEADS * HEAD_DIM
    row_spec = pl.BlockSpec((HCHUNK, w), lambda b, c: (b * nc + c, 0))
    return pl.pallas_call(
        _hgrn_body,
        grid=(batch, nc),
        in_specs=[row_spec, row_spec, row_spec, row_spec,
                  pl.BlockSpec((1, w), lambda b, c: (0, 0)),
                  pl.BlockSpec((1, w), lambda b, c: (0, 0)),
                  pl.BlockSpec((8 * HCHUNK, HCHUNK), lambda b, c: (0, 0))],
        out_specs=row_spec,
        out_shape=jax.ShapeDtypeStruct((n, w), BF16),
        scratch_shapes=[pltpu.VMEM((HGRN_HEADS // 2, LANES, LANES), F32)],
        compiler_params=_cparams("parallel", "arbitrary"),
        name="hgrn",
    )(hq, hf, hi, hg, lb, gain, masks)


def _outproj_body(x_ref, on_ref, od_ref, oh_ref, wn_ref, wd_ref, wh_ref, o_ref):
    y = _dot(on_ref[0], wn_ref[0]) + _dot(on_ref[1], wn_ref[1])
    y = y + _dot(od_ref[...], wd_ref[...]) + _dot(oh_ref[...], wh_ref[...])
    o_ref[...] = x_ref[...] + y


def _outproj(x, o_nsa, o_d, o_h, wn, wd, wh):
    n = x.shape[0]
    tm = min(TOKEN_TILE, n)
    return pl.pallas_call(
        _outproj_body,
        grid=(n // tm,),
        in_specs=[pl.BlockSpec((tm, D_MODEL), lambda i: (i, 0)),
                  pl.BlockSpec((2, tm, 192), lambda i: (0, i, 0)),
                  pl.BlockSpec((tm, 256), lambda i: (i, 0)),
                  pl.BlockSpec((tm, 384), lambda i: (i, 0)),
                  _const_spec((2, 192, D_MODEL)), _const_spec((256, D_MODEL)),
                  _const_spec((384, D_MODEL))],
        out_specs=pl.BlockSpec((tm, D_MODEL), lambda i: (i, 0)),
        out_shape=jax.ShapeDtypeStruct((n, D_MODEL), F32),
        compiler_params=_cparams("parallel"),
        name="out_proj",
    )(x, o_nsa, o_d, o_h, wn, wd, wh)


def _ple_body(x_ref, p_ref, g_ref, wg_ref, wp_ref, pg_ref, o_ref):
    x = x_ref[...]
    gate = jax.nn.sigmoid(_dot(_rms(x, g_ref[...]).astype(BF16), wg_ref[...]))
    emb = _rms(_dot(p_ref[...].astype(BF16), wp_ref[...]), pg_ref[...])
    o_ref[...] = x + gate * emb


def _ple(x, p, gain, wg, wp, pgain):
    n = x.shape[0]
    tm = min(TOKEN_TILE, n)
    return pl.pallas_call(
        _ple_body,
        grid=(n // tm,),
        in_specs=[pl.BlockSpec((tm, D_MODEL), lambda i: (i, 0)),
                  pl.BlockSpec((tm, PLE_DIM), lambda i: (i, 0)),
                  _const_spec((1, D_MODEL)), _const_spec((D_MODEL, D_MODEL)),
                  _const_spec((PLE_DIM, D_MODEL)), _const_spec((1, D_MODEL))],
        out_specs=pl.BlockSpec((tm, D_MODEL), lambda i: (i, 0)),
        out_shape=jax.ShapeDtypeStruct((n, D_MODEL), F32),
        compiler_params=_cparams("parallel"),
        name="ple",
    )(x, p, gain, wg, wp, pgain)


def _t5_bucket(dist):
    n = jnp.maximum(dist, 0)
    max_exact = NUM_BUCKETS // 2
    nf = jnp.maximum(n, 1).astype(F32)
    far = max_exact + (jnp.log(nf / max_exact) / math.log(MAX_DISTANCE / max_exact)
                       * (NUM_BUCKETS - max_exact)).astype(jnp.int32)
    return jnp.where(n < max_exact, n, jnp.minimum(far, NUM_BUCKETS - 1))


def _bias_tiles(tab, with_window):
    tab = (tab - tab[NUM_BUCKETS - 1:NUM_BUCKETS]).T * LOG2E
    d = jnp.arange(AT)[None, :] - jnp.arange(AT)[:, None]

    def lookup(dist):
        onehot = (_t5_bucket(dist)[..., None] == jnp.arange(NUM_BUCKETS)).astype(F32)
        return jnp.einsum("kqb,hb->hkq", onehot, tab, precision=HIGHEST)

    diag = jnp.where(d >= 0, lookup(d), NEG)
    near = lookup(d + AT)
    kinds = [diag, near, jnp.zeros_like(near)]
    if with_window:
        edge = jnp.where(d < 0, 0.0, NEG).astype(F32)
        kinds.append(jnp.broadcast_to(edge, near.shape))
    return jnp.stack(kinds, axis=1).astype(F32)


def _bounds_row(*vals):
    v = jnp.stack([jnp.asarray(x, F32) for x in vals])
    return jnp.pad(v, (0, LANES - v.shape[0])).reshape(1, LANES)


def _overlap_matrix_t(ncmp_pad, nslc):
    cs = np.arange(ncmp_pad)[None, :] * CMP_STRIDE
    ss = np.arange(nslc)[:, None] * SLC_BLOCK
    ov = np.maximum(np.minimum(cs + CMP_BLOCK, ss + SLC_BLOCK) - np.maximum(cs, ss), 0) / CMP_BLOCK
    return jnp.asarray(ov, BF16)


def _split_in_weights(w_in):
    bounds = np.cumsum(IN_SIZES)[:-1].tolist()
    (nq, ck, cv, sk, sv, wk, wv, ng, dq, dk, dv, hq, hf, hi, hg) = jnp.split(w_in, bounds, axis=-1)
    gcols = 3 * NSA_GROUP
    ng = jnp.concatenate(
        [jnp.pad(ng[:, g * gcols:(g + 1) * gcols], ((0, 0), (0, GATE_ROWS - gcols))) for g in range(2)]
        + [jnp.zeros((ng.shape[0], LANES - 2 * GATE_ROWS), ng.dtype)], axis=1)
    return jnp.concatenate([nq, ng, ck, cv, sk, sv, wk, wv, dq, dk, dv, hq, hf, hi, hg],
                           axis=1).astype(BF16)


def kernel(x, p, rel_bias, hgrn_lb_logits, ffn1_norm, ffn1_w_gate, ffn1_w_up, ffn1_w_down, mix_norm, w_in, w_out, nsa_q_norm, nsa_k_norm, nsa_cmp_pos, nsa_cmp_w1, nsa_cmp_w2, diff_q_norm, diff_k_norm, diff_lambda, diff_subln, hgrn_out_norm, ffn2_norm, ffn2_w_gate, ffn2_w_up, ffn2_w_down, ple_norm, ple_w_gate, ple_w_proj, ple_post_norm):
    batch, seq_len, _ = x.shape
    depth = p.shape[0]
    n = batch * seq_len
    assert seq_len % TOKEN_TILE == 0 and N_SELECT <= seq_len // SLC_BLOCK <= LANES
    nh = seq_len // CMP_STRIDE

    lb_all = jnp.cumsum(jax.nn.softmax(hgrn_lb_logits.astype(F32), axis=0), axis=0)
    lb_all = lb_all - lb_all[0:1]

    nsa_tiles = _bias_tiles(rel_bias[:, :NSA_HEADS], True)
    nsa_bias = nsa_tiles.reshape(2, NSA_GROUP, 4, AT, AT).transpose(0, 2, 3, 1, 4).reshape(
        2, 4, AT, NSA_GROUP * AT)
    diff_tiles = _bias_tiles(rel_bias[:, NSA_HEADS:], False)
    diff_bias = jnp.repeat(diff_tiles, 2, axis=0).reshape(2, 4, 3, AT, AT).transpose(
        0, 2, 3, 1, 4).reshape(2, 3, AT, 4 * AT)
    overlap_t = _overlap_matrix_t(nh, LANES)
    hgrn_masks = jnp.asarray(_hgrn_level_masks(), BF16)
    shifted_tab = (rel_bias - rel_bias[NUM_BUCKETS - 1:NUM_BUCKETS]).astype(F32) * LOG2E
    nsa_bias_max = jnp.maximum(jnp.max(shifted_tab[:, :NSA_HEADS]), 0.0)
    diff_bias_max = jnp.maximum(jnp.max(shifted_tab[:, NSA_HEADS:]), 0.0)

    xf = x.reshape(n, D_MODEL)
    pf = p.reshape(depth, n, PLE_DIM)
    row = lambda v: v.reshape(1, -1).astype(F32)
    for i in range(depth):
        xf = _ffn(xf, row(ffn1_norm[i]), ffn1_w_gate[i].astype(BF16), ffn1_w_up[i].astype(BF16),
                  ffn1_w_down[i].astype(BF16))

        qg = row(jnp.tile(nsa_q_norm[i], NSA_HEADS)) * (HEAD_DIM ** -0.5 * LOG2E)
        dqg = row(jnp.tile(diff_q_norm[i], 2 * DIFF_HEADS)) * (DIFF_QK_DIM ** -0.5 * LOG2E)
        (qt, ckv, ska, svt, wk, wvt, gates_t, dqt, dk, dvt, hq, hf, hi, hg) = _inproj(
            xf, seq_len, row(mix_norm[i]), _split_in_weights(w_in[i]), qg,
            row(jnp.tile(nsa_k_norm[i, 1], 2)), row(jnp.tile(nsa_k_norm[i, 2], 2)), dqg,
            row(jnp.tile(diff_k_norm[i], 2 * DIFF_HEADS)))

        w1 = nsa_cmp_w1[i]
        w1cat = jnp.concatenate([w1[:, :1024], w1[:, 1024:]], axis=2).astype(BF16)
        pos = jnp.broadcast_to(nsa_cmp_pos[i].reshape(2, 2, 1, 1024), (2, 2, 8, 1024))
        cmp_k, cmp_vt = _compress(ckv.reshape(4, batch, nh, 1024), pos, w1cat,
                                  nsa_cmp_w2[i].astype(BF16), row(nsa_k_norm[i, 0]))
        key_norm = lambda g, dim: NORM_SLACK * math.sqrt(dim) * jnp.max(jnp.abs(g))
        nsa_bounds = _bounds_row(key_norm(nsa_k_norm[i, 1], HEAD_DIM),
                                 key_norm(nsa_k_norm[i, 2], HEAD_DIM), nsa_bias_max)
        o_nsa = _nsa(qt, cmp_k, cmp_vt, ska, svt, wk, wvt, gates_t, nsa_bias, overlap_t,
                     nsa_bounds, batch, seq_len)

        lam_init = 0.8 - 0.6 * math.exp(-0.3 * i)
        diff_bounds = _bounds_row(key_norm(diff_k_norm[i], DIFF_QK_DIM), diff_bias_max)
        o_d = _diff(dqt, dk, dvt, diff_bias, diff_lambda[i].astype(F32),
                    row(jnp.tile(diff_subln[i], 2)), diff_bounds, lam_init, batch, seq_len)

        o_h = _hgrn(hq, hf, hi, hg, row(lb_all[i]), row(jnp.tile(hgrn_out_norm[i], HGRN_HEADS)),
                    hgrn_masks, batch, seq_len)

        wo = w_out[i].astype(BF16)
        xf = _outproj(xf, o_nsa, o_d, o_h, wo[0:384].reshape(2, 192, D_MODEL), wo[384:640],
                      wo[640:1024])
        xf = _ffn(xf, row(ffn2_norm[i]), ffn2_w_gate[i].astype(BF16), ffn2_w_up[i].astype(BF16),
                  ffn2_w_down[i].astype(BF16))
        xf = _ple(xf, pf[i], row(ple_norm[i]), ple_w_gate[i].astype(BF16),
                  ple_w_proj[i].astype(BF16), row(ple_post_norm[i]))
    return xf.reshape(batch, seq_len, D_MODEL)
```

```python
import functools
import math

import numpy as np
import jax
import jax.numpy as jnp
from jax import lax
from jax.experimental import pallas as pl
from jax.experimental.pallas import tpu as pltpu

F32 = jnp.float32
BF16 = jnp.bfloat16
HIGHEST = lax.Precision.HIGHEST

D_MODEL = 1024
D_FF = 2816
PLE_DIM = 256
HEAD_DIM = 64
NUM_BUCKETS = 32
MAX_DISTANCE = 128
NEG = -1e30
FORCE = 1e6
EPS = 1e-6
NSA_HEADS = 6
NSA_KV_HEADS = 2
NSA_GROUP = 3
CMP_BLOCK = 32
CMP_STRIDE = 16
CMP_HIDDEN = 256
SLC_BLOCK = 64
N_SELECT = 16
WINDOW = 512
DIFF_HEADS = 4
DIFF_QK_DIM = 32
HGRN_HEADS = 6
IN_SIZES = (384, 128, 128, 128, 128, 128, 128, 18, 256, 256, 256, 384, 384, 384, 384)

LANES = 128
VMEM_LIMIT = 56 * 1024 * 1024
TOKEN_TILE = 512
FF_CHUNK = 256
AT = 256
ONES_ROWS = 16
HCHUNK = 128
LOG2E = math.log2(math.e)
SHIFT_ONLY_LIMIT = 60.0
NORM_SLACK = 1.01


def _cparams(*sem):
    return pltpu.CompilerParams(dimension_semantics=sem, vmem_limit_bytes=VMEM_LIMIT)


def _const_spec(shape):
    nd = len(shape)
    return pl.BlockSpec(shape, lambda *_: (0,) * nd, pipeline_mode=pl.Buffered(1))


def _rms(x, gain):
    ms = jnp.mean(x * x, axis=-1, keepdims=True)
    return x * lax.rsqrt(ms + EPS) * gain


def _dot(a, b):
    return jnp.dot(a, b, preferred_element_type=F32)


def _dot_nt(a, b):
    return lax.dot_general(a, b, (((1,), (1,)), ((), ())), preferred_element_type=F32)


def _dot_tn(a, b):
    return lax.dot_general(a, b, (((0,), (0,)), ((), ())), preferred_element_type=F32)


def _split_bf16(x, terms):
    pieces = []
    for _ in range(terms - 1):
        hi = x.astype(BF16)
        pieces.append(hi)
        x = x - hi.astype(F32)
    pieces.append(x.astype(BF16))
    return pieces


def _dot_wide_rhs(a, x, terms):
    return sum(_dot(a, piece) for piece in _split_bf16(x, terms))


def _dot_wide_lhs(x, b, terms):
    return sum(_dot(piece, b) for piece in _split_bf16(x, terms))


def _ffn_body(x_ref, g_ref, wg_ref, wu_ref, wd_ref, o_ref):
    x = x_ref[...]
    h = _rms(x, g_ref[...]).astype(BF16)
    acc = jnp.zeros_like(x)
    for c in range(D_FF // FF_CHUNK):
        sl = slice(c * FF_CHUNK, (c + 1) * FF_CHUNK)
        a = _dot(h, wg_ref[:, sl])
        u = _dot(h, wu_ref[:, sl])
        act = (a * jax.nn.sigmoid(a) * u).astype(BF16)
        acc = acc + _dot(act, wd_ref[sl, :])
    o_ref[...] = x + 0.5 * acc


def _ffn(x, gain, wg, wu, wd):
    n = x.shape[0]
    tm = min(TOKEN_TILE, n)
    return pl.pallas_call(
        _ffn_body,
        grid=(n // tm,),
        in_specs=[pl.BlockSpec((tm, D_MODEL), lambda i: (i, 0)),
                  _const_spec((1, D_MODEL)),
                  _const_spec((D_MODEL, D_FF)),
                  _const_spec((D_MODEL, D_FF)),
                  _const_spec((D_FF, D_MODEL))],
        out_specs=pl.BlockSpec((tm, D_MODEL), lambda i: (i, 0)),
        out_shape=jax.ShapeDtypeStruct((n, D_MODEL), F32),
        compiler_params=_cparams("parallel"),
        name="ffn",
    )(x, gain, wg, wu, wd)


_C_NQ, _C_NG, _C_CK, _C_SK, _C_WK = 0, 384, 512, 768, 1024
_C_DQ, _C_DK, _C_DV = 1280, 1536, 1792
_C_HQ, _C_HI = 2048, 2816
IN_PERM_WIDTH = 3584
GATE_ROWS = 16


def _group_mean_sq(z, group):
    r = lax.broadcasted_iota(jnp.int32, (LANES, LANES), 0) // group
    c = lax.broadcasted_iota(jnp.int32, (LANES, LANES), 1) // group
    bd = jnp.where(r == c, 1.0 / group, 0.0).astype(BF16)
    zz = z * z
    parts = [_dot_wide_lhs(zz[:, j * LANES:(j + 1) * LANES], bd, 3)
             for j in range(z.shape[1] // LANES)]
    return parts[0] if len(parts) == 1 else jnp.concatenate(parts, axis=1)


def _group_norm(z, gain, group):
    return z * lax.rsqrt(_group_mean_sq(z, group) + EPS) * gain


def _store_values_t(vt_ref, v, n_groups, width):
    tm = v.shape[0]
    vt = v.T.astype(BF16)
    ones = jnp.ones((ONES_ROWS, AT), BF16)
    for g in range(n_groups):
        for c in range(tm // AT):
            vt_ref[g, c] = jnp.concatenate(
                [vt[g * width:(g + 1) * width, c * AT:(c + 1) * AT], ones], axis=0)


def _inproj_body(seq_len, x_ref, g_ref, w_ref, qg_ref, skg_ref, wkg_ref, dqg_ref, dkg_ref,
                 qt_ref, ckv_ref, ska_ref, svt_ref, wk_ref, wvt_ref, gt_ref,
                 dqt_ref, dk_ref, dvt_ref, hq_ref, hf_ref, hi_ref, hg_ref):
    tm = x_ref.shape[0]
    h = _rms(x_ref[...], g_ref[...]).astype(BF16)

    def proj(c0, width):
        return _dot(h, w_ref[:, c0:c0 + width])

    z = proj(_C_NQ, 512)
    qt_ref[...] = _group_norm(z[:, 0:384], qg_ref[...], HEAD_DIM).T.astype(BF16)
    gates_t = jax.nn.sigmoid(z[:, 384:512]).T
    for g in range(2):
        gt_ref[g] = gates_t[g * GATE_ROWS:(g + 1) * GATE_ROWS]

    ckv = proj(_C_CK, 256)
    for s in range(4):
        ckv_ref[s] = ckv[:, s * 64:(s + 1) * 64]

    z_s = proj(_C_SK, 256)
    skn = _group_norm(z_s[:, 0:LANES], skg_ref[...], HEAD_DIM).astype(BF16)
    row = lax.broadcasted_iota(jnp.int32, (tm, LANES), 0) + pl.program_id(0) * tm
    blk = (row % seq_len) // SLC_BLOCK
    col = lax.broadcasted_iota(jnp.int32, (tm, LANES), 1)
    onehot = jnp.where(blk == col, 1.0, 0.0).astype(BF16)
    ones = jnp.ones((tm, ONES_ROWS), BF16)
    for g in range(2):
        ska_ref[g] = jnp.concatenate([onehot, skn[:, g * 64:(g + 1) * 64], ones], axis=1)
    _store_values_t(svt_ref, z_s[:, LANES:2 * LANES], 2, HEAD_DIM)
    z_w = proj(_C_WK, 256)
    wkn = _group_norm(z_w[:, 0:LANES], wkg_ref[...], HEAD_DIM).astype(BF16)
    for g in range(2):
        wk_ref[g] = jnp.concatenate([wkn[:, g * 64:(g + 1) * 64], ones], axis=1)
    _store_values_t(wvt_ref, z_w[:, LANES:2 * LANES], 2, HEAD_DIM)

    dqt_ref[...] = _group_norm(proj(_C_DQ, 256), dqg_ref[...], DIFF_QK_DIM).T.astype(BF16)
    dkn = _group_norm(proj(_C_DK, 256), dkg_ref[...], DIFF_QK_DIM).astype(BF16)
    for hp in range(2):
        dk_ref[hp] = jnp.concatenate([dkn[:, hp * LANES:(hp + 1) * LANES], ones], axis=1)
    _store_values_t(dvt_ref, proj(_C_DV, 256), 2, LANES)

    z_h = proj(_C_HQ, 768)
    hq_ref[...] = z_h[:, 0:384]
    hf_ref[...] = z_h[:, 384:768]
    z_h = proj(_C_HI, 768)
    hi_ref[...] = z_h[:, 0:384]
    hg_ref[...] = z_h[:, 384:768]


def _inproj(x, seq_len, gain, w, qg, skg, wkg, dqg, dkg):
    n = x.shape[0]
    tm = min(TOKEN_TILE, seq_len)

    def rows(width, dtype):
        return (pl.BlockSpec((tm, width), lambda i: (i, 0)), jax.ShapeDtypeStruct((n, width), dtype))

    def heads(nh, width, dtype):
        return (pl.BlockSpec((nh, tm, width), lambda i: (0, i, 0)),
                jax.ShapeDtypeStruct((nh, n, width), dtype))

    def cols(height, dtype):
        return (pl.BlockSpec((height, tm), lambda i: (0, i)), jax.ShapeDtypeStruct((height, n), dtype))

    def values_t(height):
        h = height + ONES_ROWS
        return (pl.BlockSpec((2, tm // AT, h, AT), lambda i: (0, i, 0, 0)),
                jax.ShapeDtypeStruct((2, n // AT, h, AT), BF16))

    gates_t = (pl.BlockSpec((2, GATE_ROWS, tm), lambda i: (0, 0, i)),
               jax.ShapeDtypeStruct((2, GATE_ROWS, n), F32))
    outs = [cols(384, BF16), heads(4, 64, F32), heads(2, LANES + HEAD_DIM + ONES_ROWS, BF16),
            values_t(HEAD_DIM), heads(2, HEAD_DIM + ONES_ROWS, BF16), values_t(HEAD_DIM), gates_t,
            cols(256, BF16), heads(2, LANES + ONES_ROWS, BF16), values_t(LANES),
            rows(384, F32), rows(384, F32), rows(384, F32), rows(384, F32)]
    return pl.pallas_call(
        functools.partial(_inproj_body, seq_len),
        grid=(n // tm,),
        in_specs=[pl.BlockSpec((tm, D_MODEL), lambda i: (i, 0)),
                  _const_spec((1, D_MODEL)),
                  _const_spec((D_MODEL, IN_PERM_WIDTH)),
                  _const_spec((1, 384)), _const_spec((1, 128)), _const_spec((1, 128)),
                  _const_spec((1, 256)), _const_spec((1, 256))],
        out_specs=[o[0] for o in outs],
        out_shape=[o[1] for o in outs],
        compiler_params=_cparams("parallel"),
        name="in_proj",
    )(x, gain, w, qg, skg, wkg, dqg, dkg)


def _gelu_tanh(x):
    return 0.5 * x * (1.0 + jnp.tanh(math.sqrt(2.0 / math.pi) * (x + 0.044715 * (x * x * x))))


def _compress_body(h_ref, pos_ref, w1_ref, w2_ref, kg_ref, o_ref, ot_ref):
    nh = h_ref.shape[0]
    ab = _dot(h_ref[...].astype(BF16), w1_ref[...])
    pt = _dot(pos_ref[0].astype(BF16), w1_ref[...])
    pb = _dot(pos_ref[1].astype(BF16), w1_ref[...])
    bias = pt[0:1, 0:CMP_HIDDEN] + pb[0:1, CMP_HIDDEN:2 * CMP_HIDDEN]
    nxt = pltpu.roll(ab[:, CMP_HIDDEN:2 * CMP_HIDDEN], nh - 1, 0)
    hid = _gelu_tanh(ab[:, 0:CMP_HIDDEN] + nxt + bias).astype(BF16)
    out = _dot(hid, w2_ref[...])
    is_key = pl.program_id(0) < 2
    out = jnp.where(is_key, _rms(out, kg_ref[...]), out)
    o_ref[...] = out.astype(o_ref.dtype)
    out_t = jnp.concatenate([out, jnp.zeros_like(out)], axis=1).T
    ot_ref[...] = out_t[0:HEAD_DIM].astype(ot_ref.dtype)


def _compress(ckv_halves, pos, w1cat, w2, kgain):
    _, b, nh, _ = ckv_halves.shape
    return pl.pallas_call(
        _compress_body,
        grid=(4, b),
        in_specs=[pl.BlockSpec((None, None, nh, 1024), lambda s, i: (s, i, 0, 0)),
                  pl.BlockSpec((None, 2, 8, 1024), lambda s, i: (s // 2, 0, 0, 0)),
                  pl.BlockSpec((None, 1024, 2 * CMP_HIDDEN), lambda s, i: (s // 2, 0, 0)),
                  pl.BlockSpec((None, CMP_HIDDEN, HEAD_DIM), lambda s, i: (s // 2, 0, 0)),
                  pl.BlockSpec((1, HEAD_DIM), lambda s, i: (0, 0))],
        out_specs=[pl.BlockSpec((None, None, nh, HEAD_DIM), lambda s, i: (s, i, 0, 0)),
                   pl.BlockSpec((None, None, HEAD_DIM, nh), lambda s, i: (s, i, 0, 0))],
        out_shape=[jax.ShapeDtypeStruct((4, b, nh, HEAD_DIM), BF16),
                   jax.ShapeDtypeStruct((4, b, HEAD_DIM, nh), BF16)],
        compiler_params=_cparams("parallel", "parallel"),
        name="compress",
    )(ckv_halves, pos, w1cat, w2, kgain)


def _sweep(k_ref, prefix_t, q_t, k_norm_max, bias_max, vt_ref, bias_ref, kind_of, lo, hi, qi,
           m_ref, acc_ref, sa_ref, sb_ref):
    qf = q_t.astype(F32)
    bound = jnp.sqrt(jnp.sum(qf * qf, axis=0, keepdims=True)) * k_norm_max + bias_max
    srow = lax.broadcasted_iota(jnp.int32, (ONES_ROWS, q_t.shape[1]), 0)
    shift = jnp.where(srow == 0, -bound, 0.0).astype(BF16)
    parts = ([prefix_t] if prefix_t is not None else []) + [q_t, shift]
    rhs = jnp.concatenate(parts, axis=0)
    n = hi - lo

    def produce(s_ref, j):
        off = pl.multiple_of(j * AT, AT)
        s_ref[...] = _dot(k_ref[pl.ds(off, AT), :], rhs) + bias_ref[kind_of(qi - j)]

    def consume_shifted(s_ref, j):
        vt = vt_ref[j]
        for c in range(rhs.shape[1] // AT):
            cs = slice(c * AT, (c + 1) * AT)
            acc_ref[:, cs] = acc_ref[:, cs] + _dot(vt, jnp.exp2(s_ref[:, cs]).astype(BF16))

    def consume_running_max(s_ref, j):
        vt = vt_ref[j]
        for c in range(rhs.shape[1] // AT):
            cs = slice(c * AT, (c + 1) * AT)
            m_old = m_ref[:, cs]
            m_new = jnp.maximum(m_old, jnp.max(s_ref[:, cs], axis=0, keepdims=True))
            m_ref[:, cs] = m_new
            p = jnp.exp2(s_ref[:, cs] - m_new).astype(BF16)
            acc_ref[:, cs] = jnp.exp2(m_old - m_new) * acc_ref[:, cs] + _dot(vt, p)

    def run(consume, pairs_per_trip):
        acc_ref[...] = jnp.zeros(acc_ref.shape, F32)
        produce(sa_ref, lo)

        def pairs(count):
            def body(t, j):
                for _ in range(count):
                    produce(sb_ref, j + 1)
                    consume(sa_ref, j)
                    produce(sa_ref, jnp.minimum(j + 2, hi - 1))
                    consume(sb_ref, j + 1)
                    j = j + 2
                return j
            return body

        tiles = 2 * pairs_per_trip
        j = lax.fori_loop(0, n // tiles, pairs(pairs_per_trip), lo)
        if pairs_per_trip > 1:
            j = lax.fori_loop(0, (n % tiles) // 2, pairs(1), j)

        @pl.when(n % 2 == 1)
        def _():
            consume(sa_ref, hi - 1)

    shift_is_enough = jnp.max(bound) <= SHIFT_ONLY_LIMIT

    @pl.when(shift_is_enough)
    def _():
        run(consume_shifted, 2)

    @pl.when(jnp.logical_not(shift_is_enough))
    def _():
        m_ref[...] = jnp.full(m_ref.shape, NEG, F32)
        run(consume_running_max, 1)


def _nsa_body(q_ref, kc_ref, vct_ref, ska_ref, svt_ref, wk_ref, wvt_ref, gt_ref, bias_ref, ovl_ref,
              bnd_ref, o_ref, m_ref, acc_ref, sa_ref, sb_ref):
    qi = pl.program_id(2)
    q0 = qi * AT
    qblk = q_ref[...]
    q3t = jnp.concatenate([qblk[r * HEAD_DIM:(r + 1) * HEAD_DIM] for r in range(NSA_GROUP)],
                          axis=1)

    def compressed(rows):
        s_c = _dot(kc_ref[0:rows, :], q3t)
        tpos = q0 + lax.broadcasted_iota(jnp.int32, s_c.shape, 1) % AT
        cend = lax.broadcasted_iota(jnp.int32, s_c.shape, 0) * CMP_STRIDE + (CMP_BLOCK - 1)
        s_c = jnp.where(cend <= tpos, s_c, NEG)
        p_c = jnp.exp2(s_c - jnp.max(s_c, axis=0, keepdims=True))
        p_c = p_c / jnp.sum(p_c, axis=0, keepdims=True)
        p_c = jnp.where(tpos >= CMP_BLOCK - 1, p_c, 0.0)
        p_sum = p_c[:, 0:AT] + p_c[:, AT:2 * AT] + p_c[:, 2 * AT:3 * AT]
        return (_dot(vct_ref[:, 0:rows], p_c.astype(BF16)),
                _dot_wide_rhs(ovl_ref[:, 0:rows], p_sum, 3))

    ncmp = kc_ref.shape[0]
    quarter = jnp.minimum(qi // (pl.num_programs(2) // 4), 3)
    o_ct, imp_t = lax.switch(
        quarter, [functools.partial(compressed, ncmp * (k + 1) // 4) for k in range(4)])

    nslc = imp_t.shape[0]
    blk = lax.broadcasted_iota(jnp.int32, imp_t.shape, 0)
    qpos = q0 + lax.broadcasted_iota(jnp.int32, imp_t.shape, 1)
    cur = qpos // SLC_BLOCK
    forced = (blk == 0) | (blk == cur) | (blk == cur - 1)
    imp_t = jnp.where(forced, FORCE, imp_t)
    imp_t = jnp.where(blk * SLC_BLOCK > qpos, NEG, imp_t)

    def pick_one(_, carry):
        work, sel = carry
        top = jnp.max(work, axis=0, keepdims=True)
        first = jnp.min(jnp.where(work == top, blk, nslc), axis=0, keepdims=True)
        pick = blk == first
        return jnp.where(pick, -jnp.inf, work), jnp.where(pick, 1.0, sel)

    _, sel_t = lax.fori_loop(0, N_SELECT, pick_one, (imp_t, jnp.zeros_like(imp_t)))
    sel_neg = jnp.where(sel_t > 0.5, 0.0, NEG).astype(BF16)

    bounds = bnd_ref[...]
    _sweep(ska_ref, jnp.concatenate([sel_neg] * NSA_GROUP, axis=1), q3t, bounds[:, 0:1],
           bounds[:, 2:3], svt_ref, bias_ref, lambda rel: jnp.minimum(rel, 2), 0, qi + 1, qi,
           m_ref, acc_ref, sa_ref, sb_ref)
    acc = acc_ref[...]
    o_st = acc[0:HEAD_DIM] / acc[HEAD_DIM:HEAD_DIM + 1]
    _sweep(wk_ref, None, q3t, bounds[:, 1:2], bounds[:, 2:3], wvt_ref, bias_ref,
           lambda rel: jnp.where(rel == WINDOW // AT, 3, rel),
           jnp.maximum(qi - WINDOW // AT, 0), qi + 1, qi, m_ref, acc_ref, sa_ref, sb_ref)
    acc = acc_ref[...]
    o_wt = acc[0:HEAD_DIM] / acc[HEAD_DIM:HEAD_DIM + 1]

    gt = gt_ref[...]
    rows = []
    for r in range(NSA_GROUP):
        cs = slice(r * AT, (r + 1) * AT)
        rows.append(gt[3 * r:3 * r + 1] * o_ct[:, cs] + gt[3 * r + 1:3 * r + 2] * o_st[:, cs]
                    + gt[3 * r + 2:3 * r + 3] * o_wt[:, cs])
    rows.append(jnp.zeros((AT - NSA_GROUP * HEAD_DIM, AT), F32))
    out = jnp.concatenate(rows, axis=0).T
    o_ref[...] = out[:, 0:NSA_GROUP * HEAD_DIM].astype(o_ref.dtype)


def _nsa(qt, cmp_k, cmp_vt, ska, svt, wk, wvt, gates_t, bias, overlap_t, bounds, batch, seq_len):
    n = batch * seq_len
    nq = seq_len // AT
    ncmp = cmp_k.shape[2]
    nslc = overlap_t.shape[0]
    cols = NSA_GROUP * AT
    vrows = HEAD_DIM + ONES_ROWS

    def seq_spec(width):
        return pl.BlockSpec((None, seq_len, width), lambda b, g, i: (g, b, 0))

    vt_spec = pl.BlockSpec((None, nq, vrows, AT), lambda b, g, i: (g, b, 0, 0))
    return pl.pallas_call(
        _nsa_body,
        grid=(batch, 2, nq),
        in_specs=[pl.BlockSpec((NSA_GROUP * HEAD_DIM, AT), lambda b, g, i: (g, b * nq + i)),
                  pl.BlockSpec((None, None, ncmp, HEAD_DIM), lambda b, g, i: (g, b, 0, 0)),
                  pl.BlockSpec((None, None, HEAD_DIM, ncmp), lambda b, g, i: (2 + g, b, 0, 0)),
                  seq_spec(LANES + HEAD_DIM + ONES_ROWS), vt_spec, seq_spec(HEAD_DIM + ONES_ROWS),
                  vt_spec,
                  pl.BlockSpec((None, GATE_ROWS, AT), lambda b, g, i: (g, 0, b * nq + i)),
                  pl.BlockSpec((None, 4, AT, cols), lambda b, g, i: (g, 0, 0, 0)),
                  pl.BlockSpec((nslc, ncmp), lambda b, g, i: (0, 0)),
                  pl.BlockSpec((1, LANES), lambda b, g, i: (0, 0))],
        out_specs=pl.BlockSpec((None, AT, NSA_GROUP * HEAD_DIM), lambda b, g, i: (g, b * nq + i, 0)),
        out_shape=jax.ShapeDtypeStruct((2, n, NSA_GROUP * HEAD_DIM), BF16),
        scratch_shapes=[pltpu.VMEM((1, cols), F32), pltpu.VMEM((vrows, cols), F32),
                        pltpu.VMEM((AT, cols), F32), pltpu.VMEM((AT, cols), F32)],
        compiler_params=_cparams("parallel", "parallel", "arbitrary"),
        name="nsa",
    )(qt, cmp_k, cmp_vt, ska, svt, wk, wvt, gates_t, bias, overlap_t, bounds)


def _diff_body(lam_init, q_ref, k_ref, vt_ref, bias_ref, lam_ref, sg_ref, bnd_ref, o_ref, m_ref,
               acc_ref, sa_ref, sb_ref):
    qi = pl.program_id(2)
    qblk = q_ref[...]
    row = lax.broadcasted_iota(jnp.int32, qblk.shape, 0)
    zero = jnp.zeros_like(qblk)
    rhs = jnp.concatenate([jnp.where(row // DIFF_QK_DIM == j, qblk, zero) for j in range(4)], axis=1)
    bounds = bnd_ref[...]
    _sweep(k_ref, None, rhs, bounds[:, 0:1], bounds[:, 1:2], vt_ref, bias_ref,
           lambda rel: jnp.minimum(rel, 2), 0, qi + 1, qi, m_ref, acc_ref, sa_ref, sb_ref)
    acc = acc_ref[...]
    o = acc[0:LANES] / acc[LANES:LANES + 1]
    lp = lam_ref[...]
    lam = (jnp.exp(jnp.sum(lp[0:1] * lp[1:2], axis=-1, keepdims=True))
           - jnp.exp(jnp.sum(lp[2:3] * lp[3:4], axis=-1, keepdims=True)) + lam_init)
    top = row < HEAD_DIM
    d_even = o[:, 0:AT] - lam * o[:, AT:2 * AT]
    d_odd = o[:, 2 * AT:3 * AT] - lam * o[:, 3 * AT:4 * AT]
    d = jnp.where(top, d_even, d_odd)
    d2 = d * d
    ms_top = jnp.sum(jnp.where(top, d2, 0.0), axis=0, keepdims=True)
    ms_bot = jnp.sum(jnp.where(top, 0.0, d2), axis=0, keepdims=True)
    ms = jnp.where(top, ms_top, ms_bot) * (1.0 / HEAD_DIM)
    out = (d * lax.rsqrt(ms + EPS)).T * sg_ref[...] * (1.0 - lam_init)
    o_ref[...] = out.astype(o_ref.dtype)


def _diff(dqt, dk, dvt, bias, lam_p, subln, bounds, lam_init, batch, seq_len):
    n = batch * seq_len
    nq = seq_len // AT
    cols = 4 * AT
    vrows = LANES + ONES_ROWS
    return pl.pallas_call(
        functools.partial(_diff_body, lam_init),
        grid=(batch, 2, nq),
        in_specs=[pl.BlockSpec((LANES, AT), lambda b, hp, i: (hp, b * nq + i)),
                  pl.BlockSpec((None, seq_len, LANES + ONES_ROWS), lambda b, hp, i: (hp, b, 0)),
                  pl.BlockSpec((None, nq, vrows, AT), lambda b, hp, i: (hp, b, 0, 0)),
                  pl.BlockSpec((None, 3, AT, cols), lambda b, hp, i: (hp, 0, 0, 0)),
                  pl.BlockSpec((4, DIFF_QK_DIM), lambda b, hp, i: (0, 0)),
                  pl.BlockSpec((1, LANES), lambda b, hp, i: (0, 0)),
                  pl.BlockSpec((1, LANES), lambda b, hp, i: (0, 0))],
        out_specs=pl.BlockSpec((AT, LANES), lambda b, hp, i: (b * nq + i, hp)),
        out_shape=jax.ShapeDtypeStruct((n, 2 * LANES), BF16),
        scratch_shapes=[pltpu.VMEM((1, cols), F32), pltpu.VMEM((vrows, cols), F32),
                        pltpu.VMEM((AT, cols), F32), pltpu.VMEM((AT, cols), F32)],
        compiler_params=_cparams("parallel", "parallel", "arbitrary"),
        name="diff",
    )(dqt, dk, dvt, bias, lam_p, subln, bounds)


def _hgrn_level_masks():
    t = np.arange(HCHUNK)[:, None]
    j = np.arange(HCHUNK)[None, :]
    mats = [(j <= t)]
    for lvl in range(1, 8):
        bs = (2 * HCHUNK) >> lvl
        mid = (t // bs) * bs + bs // 2
        mats.append(j < mid)
    return np.concatenate(mats, axis=0).astype(np.float32)


def _hgrn_body(hq_ref, hf_ref, hi_ref, hg_ref, lb_ref, gain_ref, mask_ref, o_ref, state_ref):
    @pl.when(pl.program_id(1) == 0)
    def _():
        state_ref[...] = jnp.zeros_like(state_ref)

    fr = hf_ref[...]
    lb = lb_ref[...]
    log_sig = jnp.minimum(fr, 0.0) - jnp.log(1.0 + jnp.exp(-jnp.abs(fr)))
    a = jnp.log(lb)
    b = jnp.log1p(-lb) + log_sig
    log_f = jnp.maximum(a, b) + jnp.log(1.0 + jnp.exp(-jnp.abs(a - b)))
    kk = (1.0 - lb) * jax.nn.sigmoid(-fr)
    hq = hq_ref[...]
    qq = hq * jax.nn.sigmoid(hq)
    vv = hi_ref[...].astype(BF16)

    sums = _dot_wide_rhs(mask_ref[...], log_f, 3)
    cum = sums[0:HCHUNK]
    last = cum[HCHUNK - 1:HCHUNK]
    q_in = (qq * jnp.exp(cum)).astype(BF16)
    k_out = (kk * jnp.exp(last - cum)).astype(BF16)
    decay = jnp.exp(last)

    row = lax.broadcasted_iota(jnp.int32, (HCHUNK, 3 * LANES), 0)
    trow = lax.broadcasted_iota(jnp.int32, (2 * HCHUNK, HCHUNK), 0) % HCHUNK
    scol = lax.broadcasted_iota(jnp.int32, (2 * HCHUNK, HCHUNK), 1)
    lane = lax.broadcasted_iota(jnp.int32, (HCHUNK, LANES), 1)
    lo_half = lane < HEAD_DIM
    sq_r = lax.broadcasted_iota(jnp.int32, (LANES, LANES), 0) // HEAD_DIM
    sq_c = lax.broadcasted_iota(jnp.int32, (LANES, LANES), 1) // HEAD_DIM
    same_head = sq_r == sq_c

    def split_heads(x):
        z = jnp.zeros_like(x)
        return jnp.concatenate([jnp.where(lo_half, x, z), jnp.where(lo_half, z, x)], axis=0)

    q_lv, k_lv = [qq.astype(BF16)], [kk.astype(BF16)]
    for lvl in range(1, 8):
        bs = (2 * HCHUNK) >> lvl
        ref = sums[lvl * HCHUNK:(lvl + 1) * HCHUNK]
        upper = (row % bs) >= bs // 2
        q_lv.append((qq * jnp.exp(jnp.where(upper, cum - ref, -jnp.inf))).astype(BF16))
        k_lv.append((kk * jnp.exp(jnp.where(upper, -jnp.inf, ref - cum))).astype(BF16))

    outs = []
    for pr in range(HGRN_HEADS // 2):
        ls = slice(pr * LANES, (pr + 1) * LANES)
        scores = jnp.where(trow == scol, _dot_nt(split_heads(q_lv[0][:, ls]), k_lv[0][:, ls]), 0.0)
        for lvl in range(1, 8):
            bs = (2 * HCHUNK) >> lvl
            s_l = _dot_nt(split_heads(q_lv[lvl][:, ls]), k_lv[lvl][:, ls])
            scores = scores + jnp.where(trow // bs == scol // bs, s_l, 0.0)
        intra2 = _dot(scores.astype(BF16), vv[:, ls])
        intra = jnp.where(lo_half, intra2[0:HCHUNK], intra2[HCHUNK:2 * HCHUNK])
        st = state_ref[pr]
        inter = _dot_nt(q_in[:, ls], st.astype(BF16))
        upd = _dot_tn(vv[:, ls], k_out[:, ls])
        state_ref[pr] = st * decay[:, ls] + jnp.where(same_head, upd, 0.0)
        o = inter + intra
        o2 = o * o
        ms_lo = jnp.sum(jnp.where(lo_half, o2, 0.0), axis=-1, keepdims=True)
        ms_hi = jnp.sum(jnp.where(lo_half, 0.0, o2), axis=-1, keepdims=True)
        ms = jnp.where(lo_half, ms_lo, ms_hi) * (1.0 / HEAD_DIM)
        outs.append(o * lax.rsqrt(ms + EPS))
    hg = hg_ref[...]
    out = jnp.concatenate(outs, axis=1) * gain_ref[...] * (hg * jax.nn.sigmoid(hg))
    o_ref[...] = out.astype(o_ref.dtype)


def _hgrn(hq, hf, hi, hg, lb, gain, masks, batch, seq_len):
    n = batch * seq_len
    nc = seq_len // HCHUNK
    w = HGRN_HEADS * HEAD_DIM
    row_spec = pl.BlockSpec((HCHUNK, w), lambda b, c: (b * nc + c, 0))
    return pl.pallas_call(
        _hgrn_body,
        grid=(batch, nc),
        in_specs=[row_spec, row_spec, row_spec, row_spec,
                  pl.BlockSpec((1, w), lambda b, c: (0, 0)),
                  pl.BlockSpec((1, w), lambda b, c: (0, 0)),
                  pl.BlockSpec((8 * HCHUNK, HCHUNK), lambda b, c: (0, 0))],
        out_specs=row_spec,
        out_shape=jax.ShapeDtypeStruct((n, w), BF16),
        scratch_shapes=[pltpu.VMEM((HGRN_HEADS // 2, LANES, LANES), F32)],
        compiler_params=_cparams("parallel", "arbitrary"),
        name="hgrn",
    )(hq, hf, hi, hg, lb, gain, masks)


def _post_mixer_body(x_ref, on_ref, od_ref, oh_ref, wn_ref, wdm_ref, wh_ref,
                     fg_ref, wg_ref, wu_ref, wd_ref, p_ref, pn_ref, pwg_ref, pwp_ref, ppn_ref, o_ref):
    x = x_ref[...] + (_dot(on_ref[0], wn_ref[0]) + _dot(on_ref[1], wn_ref[1])
                      + _dot(od_ref[...], wdm_ref[...]) + _dot(oh_ref[...], wh_ref[...]))
    h = _rms(x, fg_ref[...]).astype(BF16)
    acc = jnp.zeros_like(x)
    for c in range(D_FF // FF_CHUNK):
        sl = slice(c * FF_CHUNK, (c + 1) * FF_CHUNK)
        a = _dot(h, wg_ref[:, sl])
        u = _dot(h, wu_ref[:, sl])
        act = (a * jax.nn.sigmoid(a) * u).astype(BF16)
        acc = acc + _dot(act, wd_ref[sl, :])
    x = x + 0.5 * acc
    gate = jax.nn.sigmoid(_dot(_rms(x, pn_ref[...]).astype(BF16), pwg_ref[...]))
    emb = _rms(_dot(p_ref[...].astype(BF16), pwp_ref[...]), ppn_ref[...])
    o_ref[...] = x + gate * emb


def _post_mixer(x, o_nsa, o_d, o_h, wn, wdm, wh, fgain, wg, wu, wd, p, pgain, pwg, pwp, ppgain):
    n = x.shape[0]
    tm = min(TOKEN_TILE, n)
    rows = lambda width: pl.BlockSpec((tm, width), lambda i: (i, 0))
    return pl.pallas_call(
        _post_mixer_body,
        grid=(n // tm,),
        in_specs=[rows(D_MODEL), pl.BlockSpec((2, tm, 192), lambda i: (0, i, 0)), rows(256), rows(384),
                  _const_spec((2, 192, D_MODEL)), _const_spec((256, D_MODEL)),
                  _const_spec((384, D_MODEL)),
                  _const_spec((1, D_MODEL)), _const_spec((D_MODEL, D_FF)),
                  _const_spec((D_MODEL, D_FF)), _const_spec((D_FF, D_MODEL)),
                  rows(PLE_DIM), _const_spec((1, D_MODEL)), _const_spec((D_MODEL, D_MODEL)),
                  _const_spec((PLE_DIM, D_MODEL)), _const_spec((1, D_MODEL))],
        out_specs=rows(D_MODEL),
        out_shape=jax.ShapeDtypeStruct((n, D_MODEL), F32),
        compiler_params=_cparams("parallel"),
        name="post_mixer",
    )(x, o_nsa, o_d, o_h, wn, wdm, wh, fgain, wg, wu, wd, p, pgain, pwg, pwp, ppgain)


def _t5_bucket(dist):
    n = jnp.maximum(dist, 0)
    max_exact = NUM_BUCKETS // 2
    nf = jnp.maximum(n, 1).astype(F32)
    far = max_exact + (jnp.log(nf / max_exact) / math.log(MAX_DISTANCE / max_exact)
                       * (NUM_BUCKETS - max_exact)).astype(jnp.int32)
    return jnp.where(n < max_exact, n, jnp.minimum(far, NUM_BUCKETS - 1))


def _bias_tiles(tab, with_window):
    tab = (tab - tab[NUM_BUCKETS - 1:NUM_BUCKETS]).T * LOG2E
    d = jnp.arange(AT)[None, :] - jnp.arange(AT)[:, None]

    def lookup(dist):
        onehot = (_t5_bucket(dist)[..., None] == jnp.arange(NUM_BUCKETS)).astype(F32)
        return jnp.einsum("kqb,hb->hkq", onehot, tab, precision=HIGHEST)

    diag = jnp.where(d >= 0, lookup(d), NEG)
    near = lookup(d + AT)
    kinds = [diag, near, jnp.zeros_like(near)]
    if with_window:
        edge = jnp.where(d < 0, 0.0, NEG).astype(F32)
        kinds.append(jnp.broadcast_to(edge, near.shape))
    return jnp.stack(kinds, axis=1).astype(F32)


def _bounds_row(*vals):
    v = jnp.stack([jnp.asarray(x, F32) for x in vals])
    return jnp.pad(v, (0, LANES - v.shape[0])).reshape(1, LANES)


def _overlap_matrix_t(ncmp_pad, nslc):
    cs = np.arange(ncmp_pad)[None, :] * CMP_STRIDE
    ss = np.arange(nslc)[:, None] * SLC_BLOCK
    ov = np.maximum(np.minimum(cs + CMP_BLOCK, ss + SLC_BLOCK) - np.maximum(cs, ss), 0) / CMP_BLOCK
    return jnp.asarray(ov, BF16)


def _split_in_weights(w_in):
    bounds = np.cumsum(IN_SIZES)[:-1].tolist()
    (nq, ck, cv, sk, sv, wk, wv, ng, dq, dk, dv, hq, hf, hi, hg) = jnp.split(w_in, bounds, axis=-1)
    gcols = 3 * NSA_GROUP
    ng = jnp.concatenate(
        [jnp.pad(ng[:, g * gcols:(g + 1) * gcols], ((0, 0), (0, GATE_ROWS - gcols))) for g in range(2)]
        + [jnp.zeros((ng.shape[0], LANES - 2 * GATE_ROWS), ng.dtype)], axis=1)
    return jnp.concatenate([nq, ng, ck, cv, sk, sv, wk, wv, dq, dk, dv, hq, hf, hi, hg],
                           axis=1).astype(BF16)


def kernel(x, p, rel_bias, hgrn_lb_logits, ffn1_norm, ffn1_w_gate, ffn1_w_up, ffn1_w_down, mix_norm, w_in, w_out, nsa_q_norm, nsa_k_norm, nsa_cmp_pos, nsa_cmp_w1, nsa_cmp_w2, diff_q_norm, diff_k_norm, diff_lambda, diff_subln, hgrn_out_norm, ffn2_norm, ffn2_w_gate, ffn2_w_up, ffn2_w_down, ple_norm, ple_w_gate, ple_w_proj, ple_post_norm):
    batch, seq_len, _ = x.shape
    depth = p.shape[0]
    n = batch * seq_len
    assert seq_len % TOKEN_TILE == 0 and N_SELECT <= seq_len // SLC_BLOCK <= LANES
    nh = seq_len // CMP_STRIDE

    lb_all = jnp.cumsum(jax.nn.softmax(hgrn_lb_logits.astype(F32), axis=0), axis=0)
    lb_all = lb_all - lb_all[0:1]

    nsa_tiles = _bias_tiles(rel_bias[:, :NSA_HEADS], True)
    nsa_bias = nsa_tiles.reshape(2, NSA_GROUP, 4, AT, AT).transpose(0, 2, 3, 1, 4).reshape(
        2, 4, AT, NSA_GROUP * AT)
    diff_tiles = _bias_tiles(rel_bias[:, NSA_HEADS:], False)
    diff_bias = jnp.repeat(diff_tiles, 2, axis=0).reshape(2, 4, 3, AT, AT).transpose(
        0, 2, 3, 1, 4).reshape(2, 3, AT, 4 * AT)
    overlap_t = _overlap_matrix_t(nh, LANES)
    hgrn_masks = jnp.asarray(_hgrn_level_masks(), BF16)
    shifted_tab = (rel_bias - rel_bias[NUM_BUCKETS - 1:NUM_BUCKETS]).astype(F32) * LOG2E
    nsa_bias_max = jnp.maximum(jnp.max(shifted_tab[:, :NSA_HEADS]), 0.0)
    diff_bias_max = jnp.maximum(jnp.max(shifted_tab[:, NSA_HEADS:]), 0.0)

    xf = x.reshape(n, D_MODEL)
    pf = p.reshape(depth, n, PLE_DIM)
    row = lambda v: v.reshape(1, -1).astype(F32)
    for i in range(depth):
        xf = _ffn(xf, row(ffn1_norm[i]), ffn1_w_gate[i].astype(BF16), ffn1_w_up[i].astype(BF16),
                  ffn1_w_down[i].astype(BF16))

        qg = row(jnp.tile(nsa_q_norm[i], NSA_HEADS)) * (HEAD_DIM ** -0.5 * LOG2E)
        dqg = row(jnp.tile(diff_q_norm[i], 2 * DIFF_HEADS)) * (DIFF_QK_DIM ** -0.5 * LOG2E)
        (qt, ckv, ska, svt, wk, wvt, gates_t, dqt, dk, dvt, hq, hf, hi, hg) = _inproj(
            xf, seq_len, row(mix_norm[i]), _split_in_weights(w_in[i]), qg,
            row(jnp.tile(nsa_k_norm[i, 1], 2)), row(jnp.tile(nsa_k_norm[i, 2], 2)), dqg,
            row(jnp.tile(diff_k_norm[i], 2 * DIFF_HEADS)))

        w1 = nsa_cmp_w1[i]
        w1cat = jnp.concatenate([w1[:, :1024], w1[:, 1024:]], axis=2).astype(BF16)
        pos = jnp.broadcast_to(nsa_cmp_pos[i].reshape(2, 2, 1, 1024), (2, 2, 8, 1024))
        cmp_k, cmp_vt = _compress(ckv.reshape(4, batch, nh, 1024), pos, w1cat,
                                  nsa_cmp_w2[i].astype(BF16), row(nsa_k_norm[i, 0]))
        key_norm = lambda g, dim: NORM_SLACK * math.sqrt(dim) * jnp.max(jnp.abs(g))
        nsa_bounds = _bounds_row(key_norm(nsa_k_norm[i, 1], HEAD_DIM),
                                 key_norm(nsa_k_norm[i, 2], HEAD_DIM), nsa_bias_max)
        o_nsa = _nsa(qt, cmp_k, cmp_vt, ska, svt, wk, wvt, gates_t, nsa_bias, overlap_t,
                     nsa_bounds, batch, seq_len)

        lam_init = 0.8 - 0.6 * math.exp(-0.3 * i)
        diff_bounds = _bounds_row(key_norm(diff_k_norm[i], DIFF_QK_DIM), diff_bias_max)
        o_d = _diff(dqt, dk, dvt, diff_bias, diff_lambda[i].astype(F32),
                    row(jnp.tile(diff_subln[i], 2)), diff_bounds, lam_init, batch, seq_len)

        o_h = _hgrn(hq, hf, hi, hg, row(lb_all[i]), row(jnp.tile(hgrn_out_norm[i], HGRN_HEADS)),
                    hgrn_masks, batch, seq_len)

        wo = w_out[i].astype(BF16)
        xf = _post_mixer(xf, o_nsa, o_d, o_h, wo[0:384].reshape(2, 192, D_MODEL), wo[384:640],
                         wo[640:1024], row(ffn2_norm[i]), ffn2_w_gate[i].astype(BF16),
                         ffn2_w_up[i].astype(BF16), ffn2_w_down[i].astype(BF16), pf[i],
                         row(ple_norm[i]), ple_w_gate[i].astype(BF16), ple_w_proj[i].astype(BF16),
                         row(ple_post_norm[i]))
    return xf.reshape(batch, seq_len, D_MODEL)
```

```python
import functools
import math

import numpy as np
import jax
import jax.numpy as jnp
from jax import lax
from jax.experimental import pallas as pl
from jax.experimental.pallas import tpu as pltpu

F32 = jnp.float32
BF16 = jnp.bfloat16
HIGHEST = lax.Precision.HIGHEST

D_MODEL = 1024
D_FF = 2816
PLE_DIM = 256
HEAD_DIM = 64
NUM_BUCKETS = 32
MAX_DISTANCE = 128
NEG = -1e30
FORCE = 1e6
EPS = 1e-6
NSA_HEADS = 6
NSA_KV_HEADS = 2
NSA_GROUP = 3
CMP_BLOCK = 32
CMP_STRIDE = 16
CMP_HIDDEN = 256
SLC_BLOCK = 64
N_SELECT = 16
WINDOW = 512
DIFF_HEADS = 4
DIFF_QK_DIM = 32
HGRN_HEADS = 6
IN_SIZES = (384, 128, 128, 128, 128, 128, 128, 18, 256, 256, 256, 384, 384, 384, 384)

LANES = 128
VMEM_LIMIT = 56 * 1024 * 1024
TOKEN_TILE = 512
FF_CHUNK = 256
AT = 256
ONES_ROWS = 16
HCHUNK = 128
LOG2E = math.log2(math.e)
SHIFT_ONLY_LIMIT = 60.0
NORM_SLACK = 1.01


def _cparams(*sem):
    return pltpu.CompilerParams(dimension_semantics=sem, vmem_limit_bytes=VMEM_LIMIT)


def _const_spec(shape):
    nd = len(shape)
    return pl.BlockSpec(shape, lambda *_: (0,) * nd, pipeline_mode=pl.Buffered(1))


def _rms(x, gain):
    ms = jnp.mean(x * x, axis=-1, keepdims=True)
    return x * lax.rsqrt(ms + EPS) * gain


def _dot(a, b):
    return jnp.dot(a, b, preferred_element_type=F32)


def _dot_nt(a, b):
    return lax.dot_general(a, b, (((1,), (1,)), ((), ())), preferred_element_type=F32)


def _dot_tn(a, b):
    return lax.dot_general(a, b, (((0,), (0,)), ((), ())), preferred_element_type=F32)


def _split_bf16(x, terms):
    pieces = []
    for _ in range(terms - 1):
        hi = x.astype(BF16)
        pieces.append(hi)
        x = x - hi.astype(F32)
    pieces.append(x.astype(BF16))
    return pieces


def _dot_wide_rhs(a, x, terms):
    return sum(_dot(a, piece) for piece in _split_bf16(x, terms))


def _dot_wide_lhs(x, b, terms):
    return sum(_dot(piece, b) for piece in _split_bf16(x, terms))


def _ffn_body(x_ref, g_ref, wg_ref, wu_ref, wd_ref, o_ref):
    x = x_ref[...]
    h = _rms(x, g_ref[...]).astype(BF16)
    acc = jnp.zeros_like(x)
    for c in range(D_FF // FF_CHUNK):
        sl = slice(c * FF_CHUNK, (c + 1) * FF_CHUNK)
        a = _dot(h, wg_ref[:, sl])
        u = _dot(h, wu_ref[:, sl])
        act = (a * jax.nn.sigmoid(a) * u).astype(BF16)
        acc = acc + _dot(act, wd_ref[sl, :])
    o_ref[...] = x + 0.5 * acc


def _ffn(x, gain, wg, wu, wd):
    n = x.shape[0]
    tm = min(TOKEN_TILE, n)
    return pl.pallas_call(
        _ffn_body,
        grid=(n // tm,),
        in_specs=[pl.BlockSpec((tm, D_MODEL), lambda i: (i, 0)),
                  _const_spec((1, D_MODEL)),
                  _const_spec((D_MODEL, D_FF)),
                  _const_spec((D_MODEL, D_FF)),
                  _const_spec((D_FF, D_MODEL))],
        out_specs=pl.BlockSpec((tm, D_MODEL), lambda i: (i, 0)),
        out_shape=jax.ShapeDtypeStruct((n, D_MODEL), F32),
        compiler_params=_cparams("parallel"),
        name="ffn",
    )(x, gain, wg, wu, wd)


_C_NQ, _C_NG, _C_CK, _C_SK, _C_WK = 0, 384, 512, 768, 1024
_C_DQ, _C_DK, _C_DV = 1280, 1536, 1792
_C_HQ, _C_HI = 2048, 2816
IN_PERM_WIDTH = 3584
GATE_ROWS = 16


def _group_mean_sq(z, group):
    r = lax.broadcasted_iota(jnp.int32, (LANES, LANES), 0) // group
    c = lax.broadcasted_iota(jnp.int32, (LANES, LANES), 1) // group
    bd = jnp.where(r == c, 1.0 / group, 0.0).astype(BF16)
    zz = z * z
    parts = [_dot_wide_lhs(zz[:, j * LANES:(j + 1) * LANES], bd, 3)
             for j in range(z.shape[1] // LANES)]
    return parts[0] if len(parts) == 1 else jnp.concatenate(parts, axis=1)


def _group_norm(z, gain, group):
    return z * lax.rsqrt(_group_mean_sq(z, group) + EPS) * gain


def _store_values_t(vt_ref, v, n_groups, width):
    tm = v.shape[0]
    vt = v.T.astype(BF16)
    ones = jnp.ones((ONES_ROWS, AT), BF16)
    for g in range(n_groups):
        for c in range(tm // AT):
            vt_ref[g, c] = jnp.concatenate(
                [vt[g * width:(g + 1) * width, c * AT:(c + 1) * AT], ones], axis=0)


def _inproj_body(seq_len, x_ref, g_ref, w_ref, qg_ref, skg_ref, wkg_ref, dqg_ref, dkg_ref,
                 qt_ref, ckv_ref, ska_ref, svt_ref, wk_ref, wvt_ref, gt_ref,
                 dqt_ref, dk_ref, dvt_ref, hq_ref, hf_ref, hi_ref, hg_ref):
    tm = x_ref.shape[0]
    h = _rms(x_ref[...], g_ref[...]).astype(BF16)

    def proj(c0, width):
        return _dot(h, w_ref[:, c0:c0 + width])

    z = proj(_C_NQ, 512)
    qt_ref[...] = _group_norm(z[:, 0:384], qg_ref[...], HEAD_DIM).T.astype(BF16)
    gates_t = jax.nn.sigmoid(z[:, 384:512]).T
    for g in range(2):
        gt_ref[g] = gates_t[g * GATE_ROWS:(g + 1) * GATE_ROWS]

    ckv = proj(_C_CK, 256)
    for s in range(4):
        ckv_ref[s] = ckv[:, s * 64:(s + 1) * 64]

    z_s = proj(_C_SK, 256)
    skn = _group_norm(z_s[:, 0:LANES], skg_ref[...], HEAD_DIM).astype(BF16)
    row = lax.broadcasted_iota(jnp.int32, (tm, LANES), 0) + pl.program_id(0) * tm
    blk = (row % seq_len) // SLC_BLOCK
    col = lax.broadcasted_iota(jnp.int32, (tm, LANES), 1)
    onehot = jnp.where(blk == col, 1.0, 0.0).astype(BF16)
    ones = jnp.ones((tm, ONES_ROWS), BF16)
    for g in range(2):
        ska_ref[g] = jnp.concatenate([onehot, skn[:, g * 64:(g + 1) * 64], ones], axis=1)
    _store_values_t(svt_ref, z_s[:, LANES:2 * LANES], 2, HEAD_DIM)
    z_w = proj(_C_WK, 256)
    wkn = _group_norm(z_w[:, 0:LANES], wkg_ref[...], HEAD_DIM).astype(BF16)
    for g in range(2):
        wk_ref[g] = jnp.concatenate([wkn[:, g * 64:(g + 1) * 64], ones], axis=1)
    _store_values_t(wvt_ref, z_w[:, LANES:2 * LANES], 2, HEAD_DIM)

    dqt_ref[...] = _group_norm(proj(_C_DQ, 256), dqg_ref[...], DIFF_QK_DIM).T.astype(BF16)
    dkn = _group_norm(proj(_C_DK, 256), dkg_ref[...], DIFF_QK_DIM).astype(BF16)
    for hp in range(2):
        dk_ref[hp] = jnp.concatenate([dkn[:, hp * LANES:(hp + 1) * LANES], ones], axis=1)
    _store_values_t(dvt_ref, proj(_C_DV, 256), 2, LANES)

    z_h = proj(_C_HQ, 768)
    hq_ref[...] = z_h[:, 0:384]
    hf_ref[...] = z_h[:, 384:768]
    z_h = proj(_C_HI, 768)
    hi_ref[...] = z_h[:, 0:384]
    hg_ref[...] = z_h[:, 384:768]


def _inproj(x, seq_len, gain, w, qg, skg, wkg, dqg, dkg):
    n = x.shape[0]
    tm = min(TOKEN_TILE, seq_len)

    def rows(width, dtype):
        return (pl.BlockSpec((tm, width), lambda i: (i, 0)), jax.ShapeDtypeStruct((n, width), dtype))

    def heads(nh, width, dtype):
        return (pl.BlockSpec((nh, tm, width), lambda i: (0, i, 0)),
                jax.ShapeDtypeStruct((nh, n, width), dtype))

    def cols(height, dtype):
        return (pl.BlockSpec((height, tm), lambda i: (0, i)), jax.ShapeDtypeStruct((height, n), dtype))

    def values_t(height):
        h = height + ONES_ROWS
        return (pl.BlockSpec((2, tm // AT, h, AT), lambda i: (0, i, 0, 0)),
                jax.ShapeDtypeStruct((2, n // AT, h, AT), BF16))

    gates_t = (pl.BlockSpec((2, GATE_ROWS, tm), lambda i: (0, 0, i)),
               jax.ShapeDtypeStruct((2, GATE_ROWS, n), F32))
    outs = [cols(384, BF16), heads(4, 64, F32), heads(2, LANES + HEAD_DIM + ONES_ROWS, BF16),
            values_t(HEAD_DIM), heads(2, HEAD_DIM + ONES_ROWS, BF16), values_t(HEAD_DIM), gates_t,
            cols(256, BF16), heads(2, LANES + ONES_ROWS, BF16), values_t(LANES),
            rows(384, F32), rows(384, F32), rows(384, F32), rows(384, F32)]
    return pl.pallas_call(
        functools.partial(_inproj_body, seq_len),
        grid=(n // tm,),
        in_specs=[pl.BlockSpec((tm, D_MODEL), lambda i: (i, 0)),
                  _const_spec((1, D_MODEL)),
                  _const_spec((D_MODEL, IN_PERM_WIDTH)),
                  _const_spec((1, 384)), _const_spec((1, 128)), _const_spec((1, 128)),
                  _const_spec((1, 256)), _const_spec((1, 256))],
        out_specs=[o[0] for o in outs],
        out_shape=[o[1] for o in outs],
        compiler_params=_cparams("parallel"),
        name="in_proj",
    )(x, gain, w, qg, skg, wkg, dqg, dkg)


def _gelu_tanh(x):
    return 0.5 * x * (1.0 + jnp.tanh(math.sqrt(2.0 / math.pi) * (x + 0.044715 * (x * x * x))))


def _compress_body(h_ref, pos_ref, w1_ref, w2_ref, kg_ref, o_ref, ot_ref):
    nh = h_ref.shape[0]
    ab = _dot(h_ref[...].astype(BF16), w1_ref[...])
    pt = _dot(pos_ref[0].astype(BF16), w1_ref[...])
    pb = _dot(pos_ref[1].astype(BF16), w1_ref[...])
    bias = pt[0:1, 0:CMP_HIDDEN] + pb[0:1, CMP_HIDDEN:2 * CMP_HIDDEN]
    nxt = pltpu.roll(ab[:, CMP_HIDDEN:2 * CMP_HIDDEN], nh - 1, 0)
    hid = _gelu_tanh(ab[:, 0:CMP_HIDDEN] + nxt + bias).astype(BF16)
    out = _dot(hid, w2_ref[...])
    is_key = pl.program_id(0) < 2
    out = jnp.where(is_key, _rms(out, kg_ref[...]), out)
    o_ref[...] = out.astype(o_ref.dtype)
    out_t = jnp.concatenate([out, jnp.zeros_like(out)], axis=1).T
    ot_ref[...] = out_t[0:HEAD_DIM].astype(ot_ref.dtype)


def _compress(ckv_halves, pos, w1cat, w2, kgain):
    _, b, nh, _ = ckv_halves.shape
    return pl.pallas_call(
        _compress_body,
        grid=(4, b),
        in_specs=[pl.BlockSpec((None, None, nh, 1024), lambda s, i: (s, i, 0, 0)),
                  pl.BlockSpec((None, 2, 8, 1024), lambda s, i: (s // 2, 0, 0, 0)),
                  pl.BlockSpec((None, 1024, 2 * CMP_HIDDEN), lambda s, i: (s // 2, 0, 0)),
                  pl.BlockSpec((None, CMP_HIDDEN, HEAD_DIM), lambda s, i: (s // 2, 0, 0)),
                  pl.BlockSpec((1, HEAD_DIM), lambda s, i: (0, 0))],
        out_specs=[pl.BlockSpec((None, None, nh, HEAD_DIM), lambda s, i: (s, i, 0, 0)),
                   pl.BlockSpec((None, None, HEAD_DIM, nh), lambda s, i: (s, i, 0, 0))],
        out_shape=[jax.ShapeDtypeStruct((4, b, nh, HEAD_DIM), BF16),
                   jax.ShapeDtypeStruct((4, b, HEAD_DIM, nh), BF16)],
        compiler_params=_cparams("parallel", "parallel"),
        name="compress",
    )(ckv_halves, pos, w1cat, w2, kgain)


def _sweep(k_ref, prefix_t, q_t, k_norm_max, bias_max, vt_ref, bias_ref, kind_of, lo, hi, qi,
           m_ref, acc_ref, sa_ref, sb_ref):
    qf = q_t.astype(F32)
    bound = jnp.sqrt(jnp.sum(qf * qf, axis=0, keepdims=True)) * k_norm_max + bias_max
    srow = lax.broadcasted_iota(jnp.int32, (ONES_ROWS, q_t.shape[1]), 0)
    shift = jnp.where(srow == 0, -bound, 0.0).astype(BF16)
    parts = ([prefix_t] if prefix_t is not None else []) + [q_t, shift]
    rhs = jnp.concatenate(parts, axis=0)
    n = hi - lo

    def produce(s_ref, j):
        off = pl.multiple_of(j * AT, AT)
        s_ref[...] = _dot(k_ref[pl.ds(off, AT), :], rhs) + bias_ref[kind_of(qi - j)]

    def consume_shifted(s_ref, j):
        vt = vt_ref[j]
        for c in range(rhs.shape[1] // AT):
            cs = slice(c * AT, (c + 1) * AT)
            acc_ref[:, cs] = acc_ref[:, cs] + _dot(vt, jnp.exp2(s_ref[:, cs]).astype(BF16))

    def consume_running_max(s_ref, j):
        vt = vt_ref[j]
        for c in range(rhs.shape[1] // AT):
            cs = slice(c * AT, (c + 1) * AT)
            m_old = m_ref[:, cs]
            m_new = jnp.maximum(m_old, jnp.max(s_ref[:, cs], axis=0, keepdims=True))
            m_ref[:, cs] = m_new
            p = jnp.exp2(s_ref[:, cs] - m_new).astype(BF16)
            acc_ref[:, cs] = jnp.exp2(m_old - m_new) * acc_ref[:, cs] + _dot(vt, p)

    def run(consume, pairs_per_trip):
        acc_ref[...] = jnp.zeros(acc_ref.shape, F32)
        produce(sa_ref, lo)

        def pairs(count):
            def body(t, j):
                for _ in range(count):
                    produce(sb_ref, j + 1)
                    consume(sa_ref, j)
                    produce(sa_ref, jnp.minimum(j + 2, hi - 1))
                    consume(sb_ref, j + 1)
                    j = j + 2
                return j
            return body

        j, left = lo, n
        while pairs_per_trip >= 1:
            tiles = 2 * pairs_per_trip
            j = lax.fori_loop(0, left // tiles, pairs(pairs_per_trip), j)
            left = left % tiles
            pairs_per_trip //= 2

        @pl.when(n % 2 == 1)
        def _():
            consume(sa_ref, hi - 1)

    shift_is_enough = jnp.max(bound) <= SHIFT_ONLY_LIMIT

    @pl.when(shift_is_enough)
    def _():
        run(consume_shifted, 4)

    @pl.when(jnp.logical_not(shift_is_enough))
    def _():
        m_ref[...] = jnp.full(m_ref.shape, NEG, F32)
        run(consume_running_max, 1)


def _nsa_body(q_ref, kc_ref, vct_ref, ska_ref, svt_ref, wk_ref, wvt_ref, gt_ref, bias_ref, ovl_ref,
              bnd_ref, o_ref, m_ref, acc_ref, sa_ref, sb_ref):
    qi = pl.program_id(2)
    q0 = qi * AT
    qblk = q_ref[...]
    q3t = jnp.concatenate([qblk[r * HEAD_DIM:(r + 1) * HEAD_DIM] for r in range(NSA_GROUP)],
                          axis=1)

    def compressed(rows):
        s_c = _dot(kc_ref[0:rows, :], q3t)
        tpos = q0 + lax.broadcasted_iota(jnp.int32, s_c.shape, 1) % AT
        cend = lax.broadcasted_iota(jnp.int32, s_c.shape, 0) * CMP_STRIDE + (CMP_BLOCK - 1)
        s_c = jnp.where(cend <= tpos, s_c, NEG)
        p_c = jnp.exp2(s_c - jnp.max(s_c, axis=0, keepdims=True))
        p_c = p_c / jnp.sum(p_c, axis=0, keepdims=True)
        p_c = jnp.where(tpos >= CMP_BLOCK - 1, p_c, 0.0)
        p_sum = p_c[:, 0:AT] + p_c[:, AT:2 * AT] + p_c[:, 2 * AT:3 * AT]
        return (_dot(vct_ref[:, 0:rows], p_c.astype(BF16)),
                _dot_wide_rhs(ovl_ref[:, 0:rows], p_sum, 3))

    ncmp = kc_ref.shape[0]
    quarter = jnp.minimum(qi // (pl.num_programs(2) // 4), 3)
    o_ct, imp_t = lax.switch(
        quarter, [functools.partial(compressed, ncmp * (k + 1) // 4) for k in range(4)])

    nslc = imp_t.shape[0]
    blk = lax.broadcasted_iota(jnp.int32, imp_t.shape, 0)
    qpos = q0 + lax.broadcasted_iota(jnp.int32, imp_t.shape, 1)
    cur = qpos // SLC_BLOCK
    forced = (blk == 0) | (blk == cur) | (blk == cur - 1)
    imp_t = jnp.where(forced, FORCE, imp_t)
    imp_t = jnp.where(blk * SLC_BLOCK > qpos, NEG, imp_t)

    def pick_one(_, carry):
        work, sel = carry
        top = jnp.max(work, axis=0, keepdims=True)
        first = jnp.min(jnp.where(work == top, blk, nslc), axis=0, keepdims=True)
        pick = blk == first
        return jnp.where(pick, -jnp.inf, work), jnp.where(pick, 1.0, sel)

    _, sel_t = lax.fori_loop(0, N_SELECT, pick_one, (imp_t, jnp.zeros_like(imp_t)))
    sel_neg = jnp.where(sel_t > 0.5, 0.0, NEG).astype(BF16)

    bounds = bnd_ref[...]
    _sweep(ska_ref, jnp.concatenate([sel_neg] * NSA_GROUP, axis=1), q3t, bounds[:, 0:1],
           bounds[:, 2:3], svt_ref, bias_ref, lambda rel: jnp.minimum(rel, 2), 0, qi + 1, qi,
           m_ref, acc_ref, sa_ref, sb_ref)
    acc = acc_ref[...]
    o_st = acc[0:HEAD_DIM] / acc[HEAD_DIM:HEAD_DIM + 1]
    _sweep(wk_ref, None, q3t, bounds[:, 1:2], bounds[:, 2:3], wvt_ref, bias_ref,
           lambda rel: jnp.where(rel == WINDOW // AT, 3, rel),
           jnp.maximum(qi - WINDOW // AT, 0), qi + 1, qi, m_ref, acc_ref, sa_ref, sb_ref)
    acc = acc_ref[...]
    o_wt = acc[0:HEAD_DIM] / acc[HEAD_DIM:HEAD_DIM + 1]

    gt = gt_ref[...]
    rows = []
    for r in range(NSA_GROUP):
        cs = slice(r * AT, (r + 1) * AT)
        rows.append(gt[3 * r:3 * r + 1] * o_ct[:, cs] + gt[3 * r + 1:3 * r + 2] * o_st[:, cs]
                    + gt[3 * r + 2:3 * r + 3] * o_wt[:, cs])
    rows.append(jnp.zeros((AT - NSA_GROUP * HEAD_DIM, AT), F32))
    out = jnp.concatenate(rows, axis=0).T
    o_ref[...] = out[:, 0:NSA_GROUP * HEAD_DIM].astype(o_ref.dtype)


def _nsa(qt, cmp_k, cmp_vt, ska, svt, wk, wvt, gates_t, bias, overlap_t, bounds, batch, seq_len):
    n = batch * seq_len
    nq = seq_len // AT
    ncmp = cmp_k.shape[2]
    nslc = overlap_t.shape[0]
    cols = NSA_GROUP * AT
    vrows = HEAD_DIM + ONES_ROWS

    def seq_spec(width):
        return pl.BlockSpec((None, seq_len, width), lambda b, g, i: (g, b, 0))

    vt_spec = pl.BlockSpec((None, nq, vrows, AT), lambda b, g, i: (g, b, 0, 0))
    return pl.pallas_call(
        _nsa_body,
        grid=(batch, 2, nq),
        in_specs=[pl.BlockSpec((NSA_GROUP * HEAD_DIM, AT), lambda b, g, i: (g, b * nq + i)),
                  pl.BlockSpec((None, None, ncmp, HEAD_DIM), lambda b, g, i: (g, b, 0, 0)),
                  pl.BlockSpec((None, None, HEAD_DIM, ncmp), lambda b, g, i: (2 + g, b, 0, 0)),
                  seq_spec(LANES + HEAD_DIM + ONES_ROWS), vt_spec, seq_spec(HEAD_DIM + ONES_ROWS),
                  vt_spec,
                  pl.BlockSpec((None, GATE_ROWS, AT), lambda b, g, i: (g, 0, b * nq + i)),
                  pl.BlockSpec((None, 4, AT, cols), lambda b, g, i: (g, 0, 0, 0)),
                  pl.BlockSpec((nslc, ncmp), lambda b, g, i: (0, 0)),
                  pl.BlockSpec((1, LANES), lambda b, g, i: (0, 0))],
        out_specs=pl.BlockSpec((None, AT, NSA_GROUP * HEAD_DIM), lambda b, g, i: (g, b * nq + i, 0)),
        out_shape=jax.ShapeDtypeStruct((2, n, NSA_GROUP * HEAD_DIM), BF16),
        scratch_shapes=[pltpu.VMEM((1, cols), F32), pltpu.VMEM((vrows, cols), F32),
                        pltpu.VMEM((AT, cols), F32), pltpu.VMEM((AT, cols), F32)],
        compiler_params=_cparams("parallel", "parallel", "arbitrary"),
        name="nsa",
    )(qt, cmp_k, cmp_vt, ska, svt, wk, wvt, gates_t, bias, overlap_t, bounds)


def _diff_body(lam_init, q_ref, k_ref, vt_ref, bias_ref, lam_ref, sg_ref, bnd_ref, o_ref, m_ref,
               acc_ref, sa_ref, sb_ref):
    qi = pl.program_id(2)
    qblk = q_ref[...]
    row = lax.broadcasted_iota(jnp.int32, qblk.shape, 0)
    zero = jnp.zeros_like(qblk)
    rhs = jnp.concatenate([jnp.where(row // DIFF_QK_DIM == j, qblk, zero) for j in range(4)], axis=1)
    bounds = bnd_ref[...]
    _sweep(k_ref, None, rhs, bounds[:, 0:1], bounds[:, 1:2], vt_ref, bias_ref,
           lambda rel: jnp.minimum(rel, 2), 0, qi + 1, qi, m_ref, acc_ref, sa_ref, sb_ref)
    acc = acc_ref[...]
    o = acc[0:LANES] / acc[LANES:LANES + 1]
    lp = lam_ref[...]
    lam = (jnp.exp(jnp.sum(lp[0:1] * lp[1:2], axis=-1, keepdims=True))
           - jnp.exp(jnp.sum(lp[2:3] * lp[3:4], axis=-1, keepdims=True)) + lam_init)
    top = row < HEAD_DIM
    d_even = o[:, 0:AT] - lam * o[:, AT:2 * AT]
    d_odd = o[:, 2 * AT:3 * AT] - lam * o[:, 3 * AT:4 * AT]
    d = jnp.where(top, d_even, d_odd)
    d2 = d * d
    ms_top = jnp.sum(jnp.where(top, d2, 0.0), axis=0, keepdims=True)
    ms_bot = jnp.sum(jnp.where(top, 0.0, d2), axis=0, keepdims=True)
    ms = jnp.where(top, ms_top, ms_bot) * (1.0 / HEAD_DIM)
    out = (d * lax.rsqrt(ms + EPS)).T * sg_ref[...] * (1.0 - lam_init)
    o_ref[...] = out.astype(o_ref.dtype)


def _diff(dqt, dk, dvt, bias, lam_p, subln, bounds, lam_init, batch, seq_len):
    n = batch * seq_len
    nq = seq_len // AT
    cols = 4 * AT
    vrows = LANES + ONES_ROWS
    return pl.pallas_call(
        functools.partial(_diff_body, lam_init),
        grid=(batch, 2, nq),
        in_specs=[pl.BlockSpec((LANES, AT), lambda b, hp, i: (hp, b * nq + i)),
                  pl.BlockSpec((None, seq_len, LANES + ONES_ROWS), lambda b, hp, i: (hp, b, 0)),
                  pl.BlockSpec((None, nq, vrows, AT), lambda b, hp, i: (hp, b, 0, 0)),
                  pl.BlockSpec((None, 3, AT, cols), lambda b, hp, i: (hp, 0, 0, 0)),
                  pl.BlockSpec((4, DIFF_QK_DIM), lambda b, hp, i: (0, 0)),
                  pl.BlockSpec((1, LANES), lambda b, hp, i: (0, 0)),
                  pl.BlockSpec((1, LANES), lambda b, hp, i: (0, 0))],
        out_specs=pl.BlockSpec((AT, LANES), lambda b, hp, i: (b * nq + i, hp)),
        out_shape=jax.ShapeDtypeStruct((n, 2 * LANES), BF16),
        scratch_shapes=[pltpu.VMEM((1, cols), F32), pltpu.VMEM((vrows, cols), F32),
                        pltpu.VMEM((AT, cols), F32), pltpu.VMEM((AT, cols), F32)],
        compiler_params=_cparams("parallel", "parallel", "arbitrary"),
        name="diff",
    )(dqt, dk, dvt, bias, lam_p, subln, bounds)


def _hgrn_level_masks():
    t = np.arange(HCHUNK)[:, None]
    j = np.arange(HCHUNK)[None, :]
    mats = [(j <= t)]
    for lvl in range(1, 8):
        bs = (2 * HCHUNK) >> lvl
        mid = (t // bs) * bs + bs // 2
        mats.append(j < mid)
    return np.concatenate(mats, axis=0).astype(np.float32)


def _hgrn_body(hq_ref, hf_ref, hi_ref, hg_ref, lb_ref, gain_ref, mask_ref, o_ref, state_ref):
    @pl.when(pl.program_id(1) == 0)
    def _():
        state_ref[...] = jnp.zeros_like(state_ref)

    fr = hf_ref[...]
    lb = lb_ref[...]
    log_sig = jnp.minimum(fr, 0.0) - jnp.log(1.0 + jnp.exp(-jnp.abs(fr)))
    a = jnp.log(lb)
    b = jnp.log1p(-lb) + log_sig
    log_f = jnp.maximum(a, b) + jnp.log(1.0 + jnp.exp(-jnp.abs(a - b)))
    kk = (1.0 - lb) * jax.nn.sigmoid(-fr)
    hq = hq_ref[...]
    qq = hq * jax.nn.sigmoid(hq)
    vv = hi_ref[...].astype(BF16)

    sums = _dot_wide_rhs(mask_ref[...], log_f, 3)
    cum = sums[0:HCHUNK]
    last = cum[HCHUNK - 1:HCHUNK]
    q_in = (qq * jnp.exp(cum)).astype(BF16)
    k_out = (kk * jnp.exp(last - cum)).astype(BF16)
    decay = jnp.exp(last)

    row = lax.broadcasted_iota(jnp.int32, (HCHUNK, 3 * LANES), 0)
    trow = lax.broadcasted_iota(jnp.int32, (2 * HCHUNK, HCHUNK), 0) % HCHUNK
    scol = lax.broadcasted_iota(jnp.int32, (2 * HCHUNK, HCHUNK), 1)
    lane = lax.broadcasted_iota(jnp.int32, (HCHUNK, LANES), 1)
    lo_half = lane < HEAD_DIM
    sq_r = lax.broadcasted_iota(jnp.int32, (LANES, LANES), 0) // HEAD_DIM
    sq_c = lax.broadcasted_iota(jnp.int32, (LANES, LANES), 1) // HEAD_DIM
    same_head = sq_r == sq_c

    def split_heads(x):
        z = jnp.zeros_like(x)
        return jnp.concatenate([jnp.where(lo_half, x, z), jnp.where(lo_half, z, x)], axis=0)

    q_lv, k_lv = [qq.astype(BF16)], [kk.astype(BF16)]
    for lvl in range(1, 8):
        bs = (2 * HCHUNK) >> lvl
        ref = sums[lvl * HCHUNK:(lvl + 1) * HCHUNK]
        upper = (row % bs) >= bs // 2
        q_lv.append((qq * jnp.exp(jnp.where(upper, cum - ref, -jnp.inf))).astype(BF16))
        k_lv.append((kk * jnp.exp(jnp.where(upper, -jnp.inf, ref - cum))).astype(BF16))

    outs = []
    for pr in range(HGRN_HEADS // 2):
        ls = slice(pr * LANES, (pr + 1) * LANES)
        scores = jnp.where(trow == scol, _dot_nt(split_heads(q_lv[0][:, ls]), k_lv[0][:, ls]), 0.0)
        for lvl in range(1, 8):
            bs = (2 * HCHUNK) >> lvl
            s_l = _dot_nt(split_heads(q_lv[lvl][:, ls]), k_lv[lvl][:, ls])
            scores = scores + jnp.where(trow // bs == scol // bs, s_l, 0.0)
        intra2 = _dot(scores.astype(BF16), vv[:, ls])
        intra = jnp.where(lo_half, intra2[0:HCHUNK], intra2[HCHUNK:2 * HCHUNK])
        st = state_ref[pr]
        inter = _dot_nt(q_in[:, ls], st.astype(BF16))
        upd = _dot_tn(vv[:, ls], k_out[:, ls])
        state_ref[pr] = st * decay[:, ls] + jnp.where(same_head, upd, 0.0)
        o = inter + intra
        o2 = o * o
        ms_lo = jnp.sum(jnp.where(lo_half, o2, 0.0), axis=-1, keepdims=True)
        ms_hi = jnp.sum(jnp.where(lo_half, 0.0, o2), axis=-1, keepdims=True)
        ms = jnp.where(lo_half, ms_lo, ms_hi) * (1.0 / HEAD_DIM)
        outs.append(o * lax.rsqrt(ms + EPS))
    hg = hg_ref[...]
    out = jnp.concatenate(outs, axis=1) * gain_ref[...] * (hg * jax.nn.sigmoid(hg))
    o_ref[...] = out.astype(o_ref.dtype)


def _hgrn(hq, hf, hi, hg, lb, gain, masks, batch, seq_len):
    n = batch * seq_len
    nc = seq_len // HCHUNK
    w = HGRN_HEADS * HEAD_DIM
    row_spec = pl.BlockSpec((HCHUNK, w), lambda b, c: (b * nc + c, 0))
    return pl.pallas_call(
        _hgrn_body,
        grid=(batch, nc),
        in_specs=[row_spec, row_spec, row_spec, row_spec,
                  pl.BlockSpec((1, w), lambda b, c: (0, 0)),
                  pl.BlockSpec((1, w), lambda b, c: (0, 0)),
                  pl.BlockSpec((8 * HCHUNK, HCHUNK), lambda b, c: (0, 0))],
        out_specs=row_spec,
        out_shape=jax.ShapeDtypeStruct((n, w), BF16),
        scratch_shapes=[pltpu.VMEM((HGRN_HEADS // 2, LANES, LANES), F32)],
        compiler_params=_cparams("parallel", "arbitrary"),
        name="hgrn",
    )(hq, hf, hi, hg, lb, gain, masks)


def _post_mixer_body(x_ref, on_ref, od_ref, oh_ref, wn_ref, wdm_ref, wh_ref,
                     fg_ref, wg_ref, wu_ref, wd_ref, p_ref, pn_ref, pwg_ref, pwp_ref, ppn_ref, o_ref):
    x = x_ref[...] + (_dot(on_ref[0], wn_ref[0]) + _dot(on_ref[1], wn_ref[1])
                      + _dot(od_ref[...], wdm_ref[...]) + _dot(oh_ref[...], wh_ref[...]))
    h = _rms(x, fg_ref[...]).astype(BF16)
    acc = jnp.zeros_like(x)
    for c in range(D_FF // FF_CHUNK):
        sl = slice(c * FF_CHUNK, (c + 1) * FF_CHUNK)
        a = _dot(h, wg_ref[:, sl])
        u = _dot(h, wu_ref[:, sl])
        act = (a * jax.nn.sigmoid(a) * u).astype(BF16)
        acc = acc + _dot(act, wd_ref[sl, :])
    x = x + 0.5 * acc
    gate = jax.nn.sigmoid(_dot(_rms(x, pn_ref[...]).astype(BF16), pwg_ref[...]))
    emb = _rms(_dot(p_ref[...].astype(BF16), pwp_ref[...]), ppn_ref[...])
    o_ref[...] = x + gate * emb


def _post_mixer(x, o_nsa, o_d, o_h, wn, wdm, wh, fgain, wg, wu, wd, p, pgain, pwg, pwp, ppgain):
    n = x.shape[0]
    tm = min(TOKEN_TILE, n)
    rows = lambda width: pl.BlockSpec((tm, width), lambda i: (i, 0))
    return pl.pallas_call(
        _post_mixer_body,
        grid=(n // tm,),
        in_specs=[rows(D_MODEL), pl.BlockSpec((2, tm, 192), lambda i: (0, i, 0)), rows(256), rows(384),
                  _const_spec((2, 192, D_MODEL)), _const_spec((256, D_MODEL)),
                  _const_spec((384, D_MODEL)),
                  _const_spec((1, D_MODEL)), _const_spec((D_MODEL, D_FF)),
                  _const_spec((D_MODEL, D_FF)), _const_spec((D_FF, D_MODEL)),
                  rows(PLE_DIM), _const_spec((1, D_MODEL)), _const_spec((D_MODEL, D_MODEL)),
                  _const_spec((PLE_DIM, D_MODEL)), _const_spec((1, D_MODEL))],
        out_specs=rows(D_MODEL),
        out_shape=jax.ShapeDtypeStruct((n, D_MODEL), F32),
        compiler_params=_cparams("parallel"),
        name="post_mixer",
    )(x, o_nsa, o_d, o_h, wn, wdm, wh, fgain, wg, wu, wd, p, pgain, pwg, pwp, ppgain)


def _t5_bucket(dist):
    n = jnp.maximum(dist, 0)
    max_exact = NUM_BUCKETS // 2
    nf = jnp.maximum(n, 1).astype(F32)
    far = max_exact + (jnp.log(nf / max_exact) / math.log(MAX_DISTANCE / max_exact)
                       * (NUM_BUCKETS - max_exact)).astype(jnp.int32)
    return jnp.where(n < max_exact, n, jnp.minimum(far, NUM_BUCKETS - 1))


def _bias_tiles(tab, with_window):
    tab = (tab - tab[NUM_BUCKETS - 1:NUM_BUCKETS]).T * LOG2E
    d = jnp.arange(AT)[None, :] - jnp.arange(AT)[:, None]

    def lookup(dist):
        onehot = (_t5_bucket(dist)[..., None] == jnp.arange(NUM_BUCKETS)).astype(F32)
        return jnp.einsum("kqb,hb->hkq", onehot, tab, precision=HIGHEST)

    diag = jnp.where(d >= 0, lookup(d), NEG)
    near = lookup(d + AT)
    kinds = [diag, near, jnp.zeros_like(near)]
    if with_window:
        edge = jnp.where(d < 0, 0.0, NEG).astype(F32)
        kinds.append(jnp.broadcast_to(edge, near.shape))
    return jnp.stack(kinds, axis=1).astype(F32)


def _bounds_row(*vals):
    v = jnp.stack([jnp.asarray(x, F32) for x in vals])
    return jnp.pad(v, (0, LANES - v.shape[0])).reshape(1, LANES)


def _overlap_matrix_t(ncmp_pad, nslc):
    cs = np.arange(ncmp_pad)[None, :] * CMP_STRIDE
    ss = np.arange(nslc)[:, None] * SLC_BLOCK
    ov = np.maximum(np.minimum(cs + CMP_BLOCK, ss + SLC_BLOCK) - np.maximum(cs, ss), 0) / CMP_BLOCK
    return jnp.asarray(ov, BF16)


def _split_in_weights(w_in):
    bounds = np.cumsum(IN_SIZES)[:-1].tolist()
    (nq, ck, cv, sk, sv, wk, wv, ng, dq, dk, dv, hq, hf, hi, hg) = jnp.split(w_in, bounds, axis=-1)
    gcols = 3 * NSA_GROUP
    ng = jnp.concatenate(
        [jnp.pad(ng[:, g * gcols:(g + 1) * gcols], ((0, 0), (0, GATE_ROWS - gcols))) for g in range(2)]
        + [jnp.zeros((ng.shape[0], LANES - 2 * GATE_ROWS), ng.dtype)], axis=1)
    return jnp.concatenate([nq, ng, ck, cv, sk, sv, wk, wv, dq, dk, dv, hq, hf, hi, hg],
                           axis=1).astype(BF16)


def kernel(x, p, rel_bias, hgrn_lb_logits, ffn1_norm, ffn1_w_gate, ffn1_w_up, ffn1_w_down, mix_norm, w_in, w_out, nsa_q_norm, nsa_k_norm, nsa_cmp_pos, nsa_cmp_w1, nsa_cmp_w2, diff_q_norm, diff_k_norm, diff_lambda, diff_subln, hgrn_out_norm, ffn2_norm, ffn2_w_gate, ffn2_w_up, ffn2_w_down, ple_norm, ple_w_gate, ple_w_proj, ple_post_norm):
    batch, seq_len, _ = x.shape
    depth = p.shape[0]
    n = batch * seq_len
    assert seq_len % TOKEN_TILE == 0 and N_SELECT <= seq_len // SLC_BLOCK <= LANES
    nh = seq_len // CMP_STRIDE

    lb_all = jnp.cumsum(jax.nn.softmax(hgrn_lb_logits.astype(F32), axis=0), axis=0)
    lb_all = lb_all - lb_all[0:1]

    nsa_tiles = _bias_tiles(rel_bias[:, :NSA_HEADS], True)
    nsa_bias = nsa_tiles.reshape(2, NSA_GROUP, 4, AT, AT).transpose(0, 2, 3, 1, 4).reshape(
        2, 4, AT, NSA_GROUP * AT)
    diff_tiles = _bias_tiles(rel_bias[:, NSA_HEADS:], False)
    diff_bias = jnp.repeat(diff_tiles, 2, axis=0).reshape(2, 4, 3, AT, AT).transpose(
        0, 2, 3, 1, 4).reshape(2, 3, AT, 4 * AT)
    overlap_t = _overlap_matrix_t(nh, LANES)
    hgrn_masks = jnp.asarray(_hgrn_level_masks(), BF16)
    shifted_tab = (rel_bias - rel_bias[NUM_BUCKETS - 1:NUM_BUCKETS]).astype(F32) * LOG2E
    nsa_bias_max = jnp.maximum(jnp.max(shifted_tab[:, :NSA_HEADS]), 0.0)
    diff_bias_max = jnp.maximum(jnp.max(shifted_tab[:, NSA_HEADS:]), 0.0)

    xf = x.reshape(n, D_MODEL)
    pf = p.reshape(depth, n, PLE_DIM)
    row = lambda v: v.reshape(1, -1).astype(F32)
    for i in range(depth):
        xf = _ffn(xf, row(ffn1_norm[i]), ffn1_w_gate[i].astype(BF16), ffn1_w_up[i].astype(BF16),
                  ffn1_w_down[i].astype(BF16))

        qg = row(jnp.tile(nsa_q_norm[i], NSA_HEADS)) * (HEAD_DIM ** -0.5 * LOG2E)
        dqg = row(jnp.tile(diff_q_norm[i], 2 * DIFF_HEADS)) * (DIFF_QK_DIM ** -0.5 * LOG2E)
        (qt, ckv, ska, svt, wk, wvt, gates_t, dqt, dk, dvt, hq, hf, hi, hg) = _inproj(
            xf, seq_len, row(mix_norm[i]), _split_in_weights(w_in[i]), qg,
            row(jnp.tile(nsa_k_norm[i, 1], 2)), row(jnp.tile(nsa_k_norm[i, 2], 2)), dqg,
            row(jnp.tile(diff_k_norm[i], 2 * DIFF_HEADS)))

        w1 = nsa_cmp_w1[i]
        w1cat = jnp.concatenate([w1[:, :1024], w1[:, 1024:]], axis=2).astype(BF16)
        pos = jnp.broadcast_to(nsa_cmp_pos[i].reshape(2, 2, 1, 1024), (2, 2, 8, 1024))
        cmp_k, cmp_vt = _compress(ckv.reshape(4, batch, nh, 1024), pos, w1cat,
                                  nsa_cmp_w2[i].astype(BF16), row(nsa_k_norm[i, 0]))
        key_norm = lambda g, dim: NORM_SLACK * math.sqrt(dim) * jnp.max(jnp.abs(g))
        nsa_bounds = _bounds_row(key_norm(nsa_k_norm[i, 1], HEAD_DIM),
                                 key_norm(nsa_k_norm[i, 2], HEAD_DIM), nsa_bias_max)
        o_nsa = _nsa(qt, cmp_k, cmp_vt, ska, svt, wk, wvt, gates_t, nsa_bias, overlap_t,
                     nsa_bounds, batch, seq_len)

        lam_init = 0.8 - 0.6 * math.exp(-0.3 * i)
        diff_bounds = _bounds_row(key_norm(diff_k_norm[i], DIFF_QK_DIM), diff_bias_max)
        o_d = _diff(dqt, dk, dvt, diff_bias, diff_lambda[i].astype(F32),
                    row(jnp.tile(diff_subln[i], 2)), diff_bounds, lam_init, batch, seq_len)

        o_h = _hgrn(hq, hf, hi, hg, row(lb_all[i]), row(jnp.tile(hgrn_out_norm[i], HGRN_HEADS)),
                    hgrn_masks, batch, seq_len)

        wo = w_out[i].astype(BF16)
        xf = _post_mixer(xf, o_nsa, o_d, o_h, wo[0:384].reshape(2, 192, D_MODEL), wo[384:640],
                         wo[640:1024], row(ffn2_norm[i]), ffn2_w_gate[i].astype(BF16),
                         ffn2_w_up[i].astype(BF16), ffn2_w_down[i].astype(BF16), pf[i],
                         row(ple_norm[i]), ple_w_gate[i].astype(BF16), ple_w_proj[i].astype(BF16),
                         row(ple_post_norm[i]))
    return xf.reshape(batch, seq_len, D_MODEL)
```
